```python
import jax, jax.numpy as jnp
from jax import lax
import numpy as np

D_MODEL = 2048
BATCH = 1
SEQ = 8192
DEPTH = 2

CHUNK = 64
Q_BLOCK = 2 * CHUNK
HEAD_DIM = 128
MIX_HEADS = 12
MEM_HEADS = 4
MIX_W = MIX_HEADS * HEAD_DIM
MEM_W = MEM_HEADS * HEAD_DIM
N_MEM = 256
N_MIXERS = 2
N_EXPERTS = 64
TOP_K = 8
N_GROUPS = 8
TOPK_GROUPS = 4
GROUP_TOP = 2
EXPERT_FF = 512
ROUTED_SCALE = 2.5
ALPHA = float((2 * DEPTH) ** 0.25)
BETA = float((8 * DEPTH) ** -0.25)
LN_EPS = 1e-5

kernel_name = "hybrid_stickbreak_fox_moe_deepnorm"


def _layer_norm(x, g, b):
    xf = x.astype(jnp.float32)
    mu = jnp.mean(xf, axis=-1, keepdims=True)
    var = jnp.mean(jnp.square(xf - mu), axis=-1, keepdims=True)
    y = (xf - mu) * lax.rsqrt(var + LN_EPS) * g.astype(jnp.float32) + b.astype(jnp.float32)
    return y.astype(x.dtype)


def _split_heads(t, n_heads):
    B, S, _ = t.shape
    return t.reshape(B, S, n_heads, HEAD_DIM).transpose(0, 2, 1, 3)


def _stick_breaking_attention(q, k, v):
    B, H, S, dh = q.shape
    qf = q.astype(jnp.float32) * (dh ** -0.5)
    kf = k.astype(jnp.float32)
    key_pos = jnp.arange(S)

    def block(b):
        start = b * Q_BLOCK
        qb = lax.dynamic_slice_in_dim(qf, start, Q_BLOCK, axis=2)
        z = jnp.einsum('bhqd,bhkd->bhqk', qb, kf)
        q_pos = start + jnp.arange(Q_BLOCK)
        strict = key_pos[None, :] < q_pos[:, None]
        log_stay = jnp.where(strict, jax.nn.log_sigmoid(-z), 0.0)
        log_after = lax.cumsum(log_stay, axis=3, reverse=True) - log_stay
        w = jnp.where(strict, jnp.exp(jax.nn.log_sigmoid(z) + log_after), 0.0)
        return jnp.einsum('bhqk,bhkd->bhqd', w.astype(v.dtype), v)

    out = lax.map(block, jnp.arange(S // Q_BLOCK))
    return out.transpose(1, 2, 0, 3, 4).reshape(B, H, S, dh)


def _forgetting_attention(q, k, v, log_f):
    B, H, S, dh = q.shape
    qf = q.astype(jnp.float32) * (dh ** -0.5)
    kf = k.astype(jnp.float32)
    cum_f = lax.cumsum(log_f, axis=2)
    key_pos = jnp.arange(S)

    def block(b):
        start = b * Q_BLOCK
        qb = lax.dynamic_slice_in_dim(qf, start, Q_BLOCK, axis=2)
        fq = lax.dynamic_slice_in_dim(cum_f, start, Q_BLOCK, axis=2)
        logits = jnp.einsum('bhqd,bhkd->bhqk', qb, kf) + fq[..., :, None] - cum_f[..., None, :]
        q_pos = start + jnp.arange(Q_BLOCK)
        causal = key_pos[None, :] <= q_pos[:, None]
        p = jax.nn.softmax(jnp.where(causal, logits, -jnp.inf), axis=-1)
        return jnp.einsum('bhqk,bhkd->bhqd', p.astype(v.dtype), v)

    out = lax.map(block, jnp.arange(S // Q_BLOCK))
    return out.transpose(1, 2, 0, 3, 4).reshape(B, H, S, dh)


def _memory_attention(q, k_mem, v_mem):
    s = jnp.einsum('bhqd,bhmd->bhqm', q.astype(jnp.float32), k_mem.astype(jnp.float32)) * (HEAD_DIM ** -0.5)
    p = jax.nn.softmax(s, axis=-1)
    return jnp.einsum('bhqm,bhmd->bhqd', p.astype(v_mem.dtype), v_mem)


def _attention_sublayer(x, w_in, forget_bias, w_o, k_mem, v_mem, mixer):
    B, S, _ = x.shape
    proj = x @ w_in
    q = _split_heads(proj[..., :MIX_W], MIX_HEADS)
    k = _split_heads(proj[..., MIX_W:2 * MIX_W], MIX_HEADS)
    v = _split_heads(proj[..., 2 * MIX_W:3 * MIX_W], MIX_HEADS)
    q_mem = _split_heads(proj[..., 3 * MIX_W:3 * MIX_W + MEM_W], MEM_HEADS)
    if mixer == 0:
        o_mix = _stick_breaking_attention(q, k, v)
    else:
        f_logit = proj[..., 3 * MIX_W + MEM_W:].astype(jnp.float32) + forget_bias.astype(jnp.float32)
        log_f = jax.nn.log_sigmoid(f_logit).transpose(0, 2, 1)
        o_mix = _forgetting_attention(q, k, v, log_f)
    o_mem = _memory_attention(q_mem, k_mem, v_mem)
    o = jnp.concatenate([o_mix, o_mem], axis=1)
    o = o.transpose(0, 2, 1, 3).reshape(B, S, MIX_W + MEM_W)
    return o @ w_o


def _moe(x, w_router, router_bias, exp_w_gu, exp_w_down, shared_w_gu, shared_w_down):
    B, S, D = x.shape
    xt = x.reshape(B * S, D)
    T = xt.shape[0]
    scores = jax.nn.sigmoid((xt @ w_router).astype(jnp.float32))
    sel = scores + router_bias.astype(jnp.float32)
    grouped = sel.reshape(T, N_GROUPS, N_EXPERTS // N_GROUPS)
    group_score = jnp.sum(lax.top_k(grouped, GROUP_TOP)[0], axis=-1)
    _, g_idx = lax.top_k(group_score, TOPK_GROUPS)
    group_mask = jnp.sum(jax.nn.one_hot(g_idx, N_GROUPS, dtype=jnp.float32), axis=1)
    expert_mask = jnp.repeat(group_mask, N_EXPERTS // N_GROUPS, axis=1) > 0
    _, e_idx = lax.top_k(jnp.where(expert_mask, sel, -jnp.inf), TOP_K)
    w = jnp.take_along_axis(scores, e_idx, axis=1)
    w = w / jnp.sum(w, axis=-1, keepdims=True) * ROUTED_SCALE
    gates = jnp.einsum('tke,tk->te', jax.nn.one_hot(e_idx, N_EXPERTS, dtype=jnp.float32), w).astype(x.dtype)

    def expert(acc, p):
        w_gu, w_down, g = p
        a, u = jnp.split(xt @ w_gu, 2, axis=-1)
        y = (jax.nn.silu(a) * u) @ w_down
        return acc + (g[:, None] * y).astype(acc.dtype), None

    routed, _ = lax.scan(expert, jnp.zeros_like(xt), (exp_w_gu, exp_w_down, gates.T))
    a, u = jnp.split(xt @ shared_w_gu, 2, axis=-1)
    shared = (jax.nn.silu(a) * u) @ shared_w_down
    return (routed + shared).reshape(B, S, D)


def _normal(k, shape, scale):
    return jax.random.normal(k, shape, jnp.float32) * scale


def _layer_params(key, mixer):
    ks = jax.random.split(key, 16)
    s_in = D_MODEL ** -0.5
    parts = [_normal(ks[0], (D_MODEL, MIX_W), s_in),
             _normal(ks[1], (D_MODEL, MIX_W), s_in),
             _normal(ks[2], (D_MODEL, MIX_W), s_in * BETA),
             _normal(ks[3], (D_MODEL, MEM_W), s_in)]
    if mixer == 1:
        parts.append(_normal(ks[4], (D_MODEL, MIX_HEADS), 0.5 * s_in))
    p = {'w_in': jnp.concatenate(parts, axis=1)}
    if mixer == 1:
        p['forget_bias'] = jax.random.uniform(ks[5], (MIX_HEADS,), jnp.float32, 1.0, 4.0)
    p['w_o'] = _normal(ks[6], (MIX_W + MEM_W, D_MODEL), (MIX_W + MEM_W) ** -0.5 * BETA)
    p['ln_attn_g'] = 1.0 + _normal(ks[7], (D_MODEL,), 0.02)
    p['ln_attn_b'] = _normal(ks[8], (D_MODEL,), 0.02)
    p['router'] = _normal(ks[9], (D_MODEL, N_EXPERTS), s_in)
    p['router_bias'] = _normal(ks[10], (N_EXPERTS,), 0.01)
    p['exp_w_gu'] = _normal(ks[11], (N_EXPERTS, D_MODEL, 2 * EXPERT_FF), s_in)
    p['exp_w_down'] = _normal(ks[12], (N_EXPERTS, EXPERT_FF, D_MODEL), EXPERT_FF ** -0.5 * BETA)
    p['shared_w_gu'] = _normal(ks[13], (D_MODEL, 2 * EXPERT_FF), s_in)
    p['shared_w_down'] = _normal(ks[14], (EXPERT_FF, D_MODEL), EXPERT_FF ** -0.5 * BETA)
    kg, kb = jax.random.split(ks[15])
    p['ln_ffn_g'] = 1.0 + _normal(kg, (D_MODEL,), 0.02)
    p['ln_ffn_b'] = _normal(kb, (D_MODEL,), 0.02)
    return p


def setup_inputs(seed: int = 0) -> dict:
    key = jax.random.key(seed)
    ks = jax.random.split(key, 5 + DEPTH)
    s_in = D_MODEL ** -0.5
    inputs = {
        'x': _normal(ks[0], (BATCH, SEQ, D_MODEL), 1.0),
        'mem': _normal(ks[1], (BATCH, N_MEM, D_MODEL), 1.0),
        'mem_ln_g': 1.0 + _normal(ks[2], (D_MODEL,), 0.02),
        'mem_ln_b': _normal(ks[3], (D_MODEL,), 0.02),
        'w_mem_kv': jnp.concatenate([_normal(jax.random.fold_in(ks[4], 0), (D_MODEL, MEM_W), s_in),
                                     _normal(jax.random.fold_in(ks[4], 1), (D_MODEL, MEM_W), s_in * BETA)], axis=1),
    }
    for i in range(DEPTH):
        for name, arr in _layer_params(ks[5 + i], i % N_MIXERS).items():
            inputs['l%d_%s' % (i, name)] = arr
    return inputs


def reference(x, mem, mem_ln_g, mem_ln_b, w_mem_kv,
              l0_w_in, l0_w_o, l0_ln_attn_g, l0_ln_attn_b, l0_router, l0_router_bias,
              l0_exp_w_gu, l0_exp_w_down, l0_shared_w_gu, l0_shared_w_down, l0_ln_ffn_g, l0_ln_ffn_b,
              l1_w_in, l1_forget_bias, l1_w_o, l1_ln_attn_g, l1_ln_attn_b, l1_router, l1_router_bias,
              l1_exp_w_gu, l1_exp_w_down, l1_shared_w_gu, l1_shared_w_down, l1_ln_ffn_g, l1_ln_ffn_b):
    kv = _layer_norm(mem, mem_ln_g, mem_ln_b) @ w_mem_kv
    k_mem = _split_heads(kv[..., :MEM_W], MEM_HEADS)
    v_mem = _split_heads(kv[..., MEM_W:], MEM_HEADS)

    layers = [
        (l0_w_in, None, l0_w_o, l0_ln_attn_g, l0_ln_attn_b, l0_router, l0_router_bias,
         l0_exp_w_gu, l0_exp_w_down, l0_shared_w_gu, l0_shared_w_down, l0_ln_ffn_g, l0_ln_ffn_b),
        (l1_w_in, l1_forget_bias, l1_w_o, l1_ln_attn_g, l1_ln_attn_b, l1_router, l1_router_bias,
         l1_exp_w_gu, l1_exp_w_down, l1_shared_w_gu, l1_shared_w_down, l1_ln_ffn_g, l1_ln_ffn_b),
    ]
    for i in range(DEPTH):
        (w_in, forget_bias, w_o, ln_attn_g, ln_attn_b, router, router_bias,
         exp_w_gu, exp_w_down, shared_w_gu, shared_w_down, ln_ffn_g, ln_ffn_b) = layers[i]
        mix = _attention_sublayer(x, w_in, forget_bias, w_o, k_mem, v_mem, i % N_MIXERS)
        x = _layer_norm(ALPHA * x + mix, ln_attn_g, ln_attn_b)
        ffn = _moe(x, router, router_bias, exp_w_gu, exp_w_down, shared_w_gu, shared_w_down)
        x = _layer_norm(ALPHA * x + ffn, ln_ffn_g, ln_ffn_b)
    return x
```

```python
import functools

import jax
import jax.numpy as jnp
from jax import lax
from jax.experimental import pallas as pl
from jax.experimental.pallas import tpu as pltpu

F32 = jnp.float32
BF16 = jnp.bfloat16
I32 = jnp.int32

HEAD_DIM = 128
MIX_HEADS = 12
MEM_HEADS = 4
N_EXPERTS = 64
TOP_K = 8
N_GROUPS = 8
GROUP_SIZE = N_EXPERTS // N_GROUPS
TOPK_GROUPS = 4
ROUTED_SCALE = 2.5
DEPTH = 2
ALPHA = float((2 * DEPTH) ** 0.25)
LN_EPS = 1e-5

MIB = 1024 * 1024
ATT_TILE = 256
EXPERT_TILE = 256
EXP_ZERO = -104.0
NEG_INF = float("-inf")


def _params(semantics, vmem_mib):
    return pltpu.CompilerParams(dimension_semantics=semantics, vmem_limit_bytes=vmem_mib * MIB)


def _dot(a, b):
    return jnp.dot(a, b, preferred_element_type=F32)


def _dot_nt(a, b):
    return lax.dot_general(a, b, (((1,), (1,)), ((), ())), preferred_element_type=F32)


def _split2(x):
    hi = x.astype(BF16)
    lo = (x - hi.astype(F32)).astype(BF16)
    return hi, lo


def _split3(x):
    hi = x.astype(BF16)
    r = x - hi.astype(F32)
    mid = r.astype(BF16)
    lo = (r - mid.astype(F32)).astype(BF16)
    return hi, mid, lo


def _log_sigmoid(z):
    return jnp.minimum(z, 0.0) - jnp.log(1.0 + jnp.exp(-jnp.abs(z)))


def _layer_norm(y, g, b):
    mu = jnp.mean(y, axis=-1, keepdims=True)
    d = y - mu
    var = jnp.mean(d * d, axis=-1, keepdims=True)
    return d * lax.rsqrt(var + LN_EPS) * g + b


def _kv_kernel(mem_ref, g_ref, b_ref, w_ref, o_ref):
    y = _layer_norm(mem_ref[...], g_ref[...], b_ref[...])
    o_ref[...] = _dot(y.astype(BF16), w_ref[...].astype(BF16)).astype(o_ref.dtype)


def _memory_kv(mem, g, b, w):
    n, d = mem.shape
    nw = w.shape[1]
    tn = 512
    return pl.pallas_call(
        _kv_kernel,
        grid=(nw // tn,),
        in_specs=[pl.BlockSpec((n, d), lambda j: (0, 0)),
                  pl.BlockSpec((1, d), lambda j: (0, 0)),
                  pl.BlockSpec((1, d), lambda j: (0, 0)),
                  pl.BlockSpec((d, tn), lambda j: (0, j))],
        out_specs=pl.BlockSpec((n, tn), lambda j: (0, j)),
        out_shape=jax.ShapeDtypeStruct((n, nw), BF16),
        compiler_params=_params(("arbitrary",), 32),
        name="memory_kv",
    )(mem, g.reshape(1, d), b.reshape(1, d), w)


def _in_proj_kernel(x_ref, w_ref, o_ref, wbf_ref):
    @pl.when(pl.program_id(1) == 0)
    def _():
        wbf_ref[...] = w_ref[...].astype(BF16)

    o_ref[...] = _dot(x_ref[...], wbf_ref[...]).astype(o_ref.dtype)


def _in_proj(xb, w, n_out):
    m, d = xb.shape
    tm, tn = 1024, 512
    tm = min(tm, m)
    return pl.pallas_call(
        _in_proj_kernel,
        grid=(n_out // tn, m // tm),
        in_specs=[pl.BlockSpec((tm, d), lambda j, i: (i, 0)),
                  pl.BlockSpec((d, tn), lambda j, i: (0, j))],
        out_specs=pl.BlockSpec((tm, tn), lambda j, i: (i, j)),
        out_shape=jax.ShapeDtypeStruct((m, n_out), BF16),
        scratch_shapes=[pltpu.VMEM((d, tn), BF16)],
        compiler_params=_params(("arbitrary", "arbitrary"), 40),
        name="in_proj",
    )(xb, w)


def _gate_kernel(x_ref, w_ref, b_ref, cum_ref, cum_t_ref, carry_ref):
    tm = x_ref.shape[0]

    @pl.when(pl.program_id(0) == 0)
    def _():
        carry_ref[...] = jnp.zeros_like(carry_ref)

    xh, xl = _split2(x_ref[...])
    wh, wl = _split2(w_ref[...])
    f = _dot(xh, wh) + _dot(xh, wl) + _dot(xl, wh) + b_ref[...]
    lf = _log_sigmoid(f)
    row = lax.broadcasted_iota(I32, (tm, tm), 0)
    col = lax.broadcasted_iota(I32, (tm, tm), 1)
    lower = (col <= row).astype(BF16)
    p0, p1, p2 = _split3(lf)
    cum = _dot(lower, p0) + _dot(lower, p1) + _dot(lower, p2) + carry_ref[...]
    carry_ref[...] = cum[tm - 1:tm, :]
    cum_ref[...] = cum
    cum_t_ref[...] = cum.T[:cum_t_ref.shape[0], :]


def _forget_cumsum(x, w_f, bias):
    t, d = x.shape
    h = w_f.shape[1]
    tm = min(256, t)
    w_pad = jnp.zeros((d, HEAD_DIM), F32).at[:, :h].set(w_f)
    b_pad = jnp.zeros((1, HEAD_DIM), F32).at[0, :h].set(bias)
    return pl.pallas_call(
        _gate_kernel,
        grid=(t // tm,),
        in_specs=[pl.BlockSpec((tm, d), lambda i: (i, 0)),
                  pl.BlockSpec((d, HEAD_DIM), lambda i: (0, 0)),
                  pl.BlockSpec((1, HEAD_DIM), lambda i: (0, 0))],
        out_specs=[pl.BlockSpec((tm, HEAD_DIM), lambda i: (i, 0)),
                   pl.BlockSpec((16, tm), lambda i: (0, i))],
        out_shape=[jax.ShapeDtypeStruct((t, HEAD_DIM), F32),
                   jax.ShapeDtypeStruct((16, t), F32)],
        scratch_shapes=[pltpu.VMEM((1, HEAD_DIM), F32)],
        compiler_params=_params(("arbitrary",), 32),
        name="forget_cumsum",
    )(x, w_pad, b_pad)


def _stick_kernel(q_ref, k_ref, v_ref, o_ref, *, scale):
    tq = q_ref.shape[0]
    i = pl.program_id(1)
    q = q_ref[...]
    row = lax.broadcasted_iota(I32, (tq, tq), 0)
    col = lax.broadcasted_iota(I32, (tq, tq), 1)
    strict = col < row
    later = (row > col).astype(BF16)

    def block(j, c, acc, masked):
        start = pl.multiple_of(j * tq, tq)
        kb = k_ref[pl.ds(start, tq), :]
        vb = v_ref[pl.ds(start, tq), :]
        z = _dot_nt(q, kb) * scale
        soft = jnp.log(1.0 + jnp.exp(-jnp.abs(z)))
        log_beta = jnp.minimum(z, 0.0) - soft
        log_stay = jnp.minimum(-z, 0.0) - soft
        if masked:
            log_stay = jnp.where(strict, log_stay, 0.0)
        hi, lo = _split2(log_stay)
        log_after = _dot(hi, later) + _dot(lo, later) + c
        w = jnp.exp(log_beta + log_after)
        if masked:
            w = jnp.where(strict, w, 0.0)
        acc = acc + _dot(w.astype(BF16), vb)
        c = c + jnp.sum(log_stay, axis=1, keepdims=True)
        return c, acc

    c0 = jnp.zeros((tq, 1), F32)
    acc0 = jnp.zeros((tq, HEAD_DIM), F32)
    c1, acc1 = block(i, c0, acc0, True)

    def cond(state):
        j, alive, _, _ = state
        return jnp.logical_and(j >= 0, alive > 0)

    def body(state):
        j, _, c, acc = state
        c, acc = block(j, c, acc, False)
        alive = (jnp.max(c) > EXP_ZERO).astype(I32)
        return j - 1, alive, c, acc

    alive1 = (jnp.max(c1) > EXP_ZERO).astype(I32)
    _, _, _, acc = lax.while_loop(cond, body, (i - 1, alive1, c1, acc1))
    o_ref[...] = acc.astype(o_ref.dtype)


def _stick_attention(proj, n_heads):
    t = proj.shape[0]
    tq = min(ATT_TILE, t)
    kern = functools.partial(_stick_kernel, scale=HEAD_DIM ** -0.5)
    return pl.pallas_call(
        kern,
        grid=(n_heads, t // tq),
        in_specs=[pl.BlockSpec((tq, HEAD_DIM), lambda h, i: (i, h)),
                  pl.BlockSpec((t, HEAD_DIM), lambda h, i: (0, n_heads + h)),
                  pl.BlockSpec((t, HEAD_DIM), lambda h, i: (0, 2 * n_heads + h))],
        out_specs=pl.BlockSpec((tq, HEAD_DIM), lambda h, i: (i, h)),
        out_shape=jax.ShapeDtypeStruct((t, n_heads * HEAD_DIM), BF16),
        compiler_params=_params(("arbitrary", "arbitrary"), 40),
        name="stick_attention",
    )(proj, proj, proj)


def _fox_kernel(q_ref, k_ref, v_ref, fq_ref, fk_ref, o_ref, *, scale):
    tq = q_ref.shape[0]
    h = pl.program_id(0)
    i = pl.program_id(1)
    q = q_ref[...]
    lane = lax.broadcasted_iota(I32, fq_ref.shape, 1)
    fq = jnp.sum(jnp.where(lane == h, fq_ref[...], 0.0), axis=1, keepdims=True)
    row = lax.broadcasted_iota(I32, (tq, tq), 0)
    col = lax.broadcasted_iota(I32, (tq, tq), 1)
    causal = col <= row

    def block(j, m, l, acc, masked):
        start = pl.multiple_of(j * tq, tq)
        kb = k_ref[pl.ds(start, tq), :]
        vb = v_ref[pl.ds(start, tq), :]
        fk = fk_ref[0, j]
        s = _dot_nt(q, kb) * scale + (fq - fk)
        if masked:
            s = jnp.where(causal, s, NEG_INF)
        m_new = jnp.maximum(m, jnp.max(s, axis=1, keepdims=True))
        alpha = jnp.exp(m - m_new)
        p = jnp.exp(s - m_new)
        l = alpha * l + jnp.sum(p, axis=1, keepdims=True)
        acc = alpha * acc + _dot(p.astype(BF16), vb)
        return m_new, l, acc

    m0 = jnp.full((tq, 1), NEG_INF, F32)
    l0 = jnp.zeros((tq, 1), F32)
    acc0 = jnp.zeros((tq, HEAD_DIM), F32)
    state = block(i, m0, l0, acc0, True)

    def body(n, state):
        return block(i - 1 - n, *state, False)

    _, l, acc = lax.fori_loop(0, i, body, state)
    o_ref[...] = (acc * (1.0 / l)).astype(o_ref.dtype)


def _fox_attention(proj, cum, cum_t, n_heads):
    t = proj.shape[0]
    tq = min(ATT_TILE, t)
    fk = cum_t.reshape(cum_t.shape[0], t // tq, 1, tq)
    kern = functools.partial(_fox_kernel, scale=HEAD_DIM ** -0.5)
    return pl.pallas_call(
        kern,
        grid=(n_heads, t // tq),
        in_specs=[pl.BlockSpec((tq, HEAD_DIM), lambda h, i: (i, h)),
                  pl.BlockSpec((t, HEAD_DIM), lambda h, i: (0, n_heads + h)),
                  pl.BlockSpec((t, HEAD_DIM), lambda h, i: (0, 2 * n_heads + h)),
                  pl.BlockSpec((tq, HEAD_DIM), lambda h, i: (i, 0)),
                  pl.BlockSpec((1, t // tq, 1, tq), lambda h, i: (h, 0, 0, 0))],
        out_specs=pl.BlockSpec((tq, HEAD_DIM), lambda h, i: (i, h)),
        out_shape=jax.ShapeDtypeStruct((t, n_heads * HEAD_DIM), BF16),
        compiler_params=_params(("arbitrary", "arbitrary"), 40),
        name="fox_attention",
    )(proj, proj, proj, cum, fk)


def _mem_attn_kernel(q_ref, k_ref, v_ref, o_ref, *, scale):
    s = _dot_nt(q_ref[...], k_ref[...]) * scale
    m = jnp.max(s, axis=1, keepdims=True)
    p = jnp.exp(s - m)
    l = jnp.sum(p, axis=1, keepdims=True)
    o_ref[...] = (_dot(p.astype(BF16), v_ref[...]) * (1.0 / l)).astype(o_ref.dtype)


def _memory_attention(proj, kv, q_col0):
    t = proj.shape[0]
    n_mem = kv.shape[0]
    tq = min(1024, t)
    c0 = q_col0 // HEAD_DIM
    kern = functools.partial(_mem_attn_kernel, scale=HEAD_DIM ** -0.5)
    return pl.pallas_call(
        kern,
        grid=(MEM_HEADS, t // tq),
        in_specs=[pl.BlockSpec((tq, HEAD_DIM), lambda h, i: (i, c0 + h)),
                  pl.BlockSpec((n_mem, HEAD_DIM), lambda h, i: (0, h)),
                  pl.BlockSpec((n_mem, HEAD_DIM), lambda h, i: (0, MEM_HEADS + h))],
        out_specs=pl.BlockSpec((tq, HEAD_DIM), lambda h, i: (i, h)),
        out_shape=jax.ShapeDtypeStruct((t, MEM_HEADS * HEAD_DIM), BF16),
        compiler_params=_params(("arbitrary", "arbitrary"), 32),
        name="memory_attention",
    )(proj, kv, kv)


def _out_proj_kernel(om_ref, oc_ref, w_ref, x_ref, g_ref, b_ref, y_ref):
    n_mix = om_ref.shape[1]
    mix = _dot(om_ref[...], w_ref[:n_mix, :]) + _dot(oc_ref[...], w_ref[n_mix:, :])
    y_ref[...] = _layer_norm(ALPHA * x_ref[...] + mix, g_ref[...], b_ref[...])


def _out_proj_norm(o_mix, o_mem, w_o_bf, x, g, b):
    t, d = x.shape
    tm = min(512, t)
    n_mix, n_mem = o_mix.shape[1], o_mem.shape[1]
    return pl.pallas_call(
        _out_proj_kernel,
        grid=(t // tm,),
        in_specs=[pl.BlockSpec((tm, n_mix), lambda i: (i, 0)),
                  pl.BlockSpec((tm, n_mem), lambda i: (i, 0)),
                  pl.BlockSpec((n_mix + n_mem, d), lambda i: (0, 0)),
                  pl.BlockSpec((tm, d), lambda i: (i, 0)),
                  pl.BlockSpec((1, d), lambda i: (0, 0)),
                  pl.BlockSpec((1, d), lambda i: (0, 0))],
        out_specs=pl.BlockSpec((tm, d), lambda i: (i, 0)),
        out_shape=jax.ShapeDtypeStruct((t, d), F32),
        compiler_params=_params(("arbitrary",), 48),
        name="out_proj_norm",
    )(o_mix, o_mem, w_o_bf, x, g.reshape(1, d), b.reshape(1, d))


def _router_kernel(x_ref, wh_ref, wl_ref, b_ref, eid_ref, rank_ref, gate_ref, cnt_ref, carry_ref):
    tm = x_ref.shape[0]

    @pl.when(pl.program_id(0) == 0)
    def _():
        carry_ref[...] = jnp.zeros_like(carry_ref)

    xh, xl = _split2(x_ref[...])
    wh, wl = wh_ref[...], wl_ref[...]
    logits = _dot_nt(wh, xh) + _dot_nt(wh, xl) + _dot_nt(wl, xh)
    scores = 1.0 / (1.0 + jnp.exp(-logits))
    sel = scores + b_ref[:, 0:1]

    group_score = []
    for g in range(N_GROUPS):
        v = sel[g * GROUP_SIZE:(g + 1) * GROUP_SIZE, :]
        m1 = jnp.max(v, axis=0, keepdims=True)
        is_max = v == m1
        n_max = jnp.sum(is_max.astype(F32), axis=0, keepdims=True)
        m2 = jnp.max(jnp.where(is_max, NEG_INF, v), axis=0, keepdims=True)
        group_score.append(m1 + jnp.where(n_max >= 2.0, m1, m2))
    masked = []
    for g in range(N_GROUPS):
        ahead = jnp.zeros((1, tm), F32)
        for g2 in range(N_GROUPS):
            if g2 == g:
                continue
            beats = group_score[g2] > group_score[g]
            if g2 < g:
                beats = jnp.logical_or(beats, group_score[g2] == group_score[g])
            ahead = ahead + beats.astype(F32)
        v = sel[g * GROUP_SIZE:(g + 1) * GROUP_SIZE, :]
        masked.append(jnp.where(ahead < float(TOPK_GROUPS), v, NEG_INF))
    msel = jnp.concatenate(masked, axis=0)

    e_idx = lax.broadcasted_iota(I32, (N_EXPERTS, tm), 0)
    ahead = jnp.zeros((N_EXPERTS, tm), F32)
    for e2 in range(N_EXPERTS):
        other = msel[e2:e2 + 1, :]
        beats = jnp.logical_or(other > msel, jnp.logical_and(other == msel, e_idx > e2))
        ahead = ahead + beats.astype(F32)
    chosen = ahead < float(TOP_K)
    chosen_f = chosen.astype(F32)

    w = jnp.where(chosen, scores, 0.0)
    gates = w / jnp.sum(w, axis=0, keepdims=True) * ROUTED_SCALE

    chosen_b = chosen_f.astype(BF16)
    r64 = lax.broadcasted_iota(I32, (N_EXPERTS, N_EXPERTS), 0)
    c64 = lax.broadcasted_iota(I32, (N_EXPERTS, N_EXPERTS), 1)
    choice = _dot((c64 < r64).astype(BF16), chosen_b)
    rt = lax.broadcasted_iota(I32, (tm, tm), 0)
    ct = lax.broadcasted_iota(I32, (tm, tm), 1)
    rank = _dot(chosen_b, (rt < ct).astype(BF16)) + carry_ref[:, 0:1]
    carry_ref[...] = carry_ref[...] + jnp.sum(chosen_f, axis=1, keepdims=True)
    cnt_ref[...] = carry_ref[...].astype(I32)

    e_f = e_idx.astype(F32)
    eids, ranks, gts = [], [], []
    for k in range(TOP_K):
        pick = jnp.logical_and(chosen, choice == float(k))
        eids.append(jnp.sum(jnp.where(pick, e_f, 0.0), axis=0, keepdims=True))
        ranks.append(jnp.sum(jnp.where(pick, rank, 0.0), axis=0, keepdims=True))
        gts.append(jnp.sum(jnp.where(pick, gates, 0.0), axis=0, keepdims=True))
    eid_ref[...] = jnp.concatenate(eids, axis=0).astype(I32)
    rank_ref[...] = jnp.concatenate(ranks, axis=0).astype(I32)
    gate_ref[...] = jnp.concatenate(gts, axis=0)


def _router(x1, w_router, router_bias):
    t, d = x1.shape
    tm = min(512, t)
    wt = w_router.T
    wh, wl = _split2(wt)
    bias = jnp.broadcast_to(router_bias.astype(F32)[:, None], (N_EXPERTS, HEAD_DIM))
    return pl.pallas_call(
        _router_kernel,
        grid=(t // tm,),
        in_specs=[pl.BlockSpec((tm, d), lambda i: (i, 0)),
                  pl.BlockSpec((N_EXPERTS, d), lambda i: (0, 0)),
                  pl.BlockSpec((N_EXPERTS, d), lambda i: (0, 0)),
                  pl.BlockSpec((N_EXPERTS, HEAD_DIM), lambda i: (0, 0))],
        out_specs=[pl.BlockSpec((TOP_K, tm), lambda i: (0, i)),
                   pl.BlockSpec((TOP_K, tm), lambda i: (0, i)),
                   pl.BlockSpec((TOP_K, tm), lambda i: (0, i)),
                   pl.BlockSpec((N_EXPERTS, HEAD_DIM), lambda i: (0, 0))],
        out_shape=[jax.ShapeDtypeStruct((TOP_K, t), I32),
                   jax.ShapeDtypeStruct((TOP_K, t), I32),
                   jax.ShapeDtypeStruct((TOP_K, t), F32),
                   jax.ShapeDtypeStruct((N_EXPERTS, HEAD_DIM), I32)],
        scratch_shapes=[pltpu.VMEM((N_EXPERTS, HEAD_DIM), F32)],
        compiler_params=_params(("arbitrary",), 40),
        name="router",
    )(x1, wh, wl, bias)


def _expert_kernel(te_ref, nv_ref, nu_ref, meta_hbm, x_hbm, wgu_ref, wd_ref, y_hbm,
                   meta_s, xbuf, ybuf, wgu_bf, wd_bf, msem, gsem, ssem, *, n_tok, tok_bits):
    tm = xbuf.shape[1]
    ff = wd_ref.shape[0]
    j = pl.program_id(0)
    nu = nu_ref[0]
    tok_mask = (1 << tok_bits) - 1

    def meta_copy(tile, slot):
        return pltpu.make_async_copy(meta_hbm.at[pl.ds(tile * tm, tm)], meta_s.at[slot], msem.at[slot])

    def gather_row(r, mslot, bslot):
        tok = meta_s[mslot, r] & tok_mask
        return pltpu.make_async_copy(x_hbm.at[pl.ds(tok, 1), :], xbuf.at[bslot, pl.ds(r, 1), :],
                                     gsem.at[bslot])

    def scatter_row(r, mslot, bslot):
        meta = meta_s[mslot, r]
        dest = (meta >> tok_bits) * n_tok + (meta & tok_mask)
        return pltpu.make_async_copy(ybuf.at[bslot, pl.ds(r, 1), :], y_hbm.at[pl.ds(dest, 1), :],
                                     ssem.at[bslot])

    def start_gathers(nv, mslot, bslot):
        def one(r, _):
            gather_row(r, mslot, bslot).start()
            return 0
        lax.fori_loop(0, nv, one, 0)

    def wait_rows(nv, bulk_copy, row_copy):
        n8 = pl.multiple_of((nv // 8) * 8, 8)

        @pl.when(n8 > 0)
        def _():
            bulk_copy(n8).wait()

        def one(r, _):
            row_copy(r).wait()
            return 0
        lax.fori_loop(n8, nv, one, 0)

    def wait_gathers(nv, bslot):
        wait_rows(nv,
                  lambda n: pltpu.make_async_copy(x_hbm.at[pl.ds(0, n), :],
                                                  xbuf.at[bslot, pl.ds(0, n), :], gsem.at[bslot]),
                  lambda r: pltpu.make_async_copy(x_hbm.at[pl.ds(0, 1), :],
                                                  xbuf.at[bslot, pl.ds(r, 1), :], gsem.at[bslot]))

    def start_scatters(nv, mslot, bslot):
        def one(r, _):
            scatter_row(r, mslot, bslot).start()
            return 0
        lax.fori_loop(0, nv, one, 0)

    def wait_scatters(nv, bslot):
        wait_rows(nv,
                  lambda n: pltpu.make_async_copy(ybuf.at[bslot, pl.ds(0, n), :],
                                                  y_hbm.at[pl.ds(0, n), :], ssem.at[bslot]),
                  lambda r: pltpu.make_async_copy(ybuf.at[bslot, pl.ds(r, 1), :],
                                                  y_hbm.at[pl.ds(0, 1), :], ssem.at[bslot]))

    @pl.when(j == 0)
    def _():
        xbuf[...] = jnp.zeros_like(xbuf)
        meta_copy(0, 0).start()
        meta_copy(0, 0).wait()
        start_gathers(nv_ref[0], 0, 0)

        @pl.when(nu > 1)
        def _():
            meta_copy(1, 1).start()

    @pl.when(j < nu)
    def _():
        bslot = j % 2
        mslot = j % 3
        nv = nv_ref[j]

        @pl.when(j + 2 < nu)
        def _():
            meta_copy(j + 2, (j + 2) % 3).start()

        @pl.when(j + 1 < nu)
        def _():
            meta_copy(j + 1, (j + 1) % 3).wait()
            start_gathers(nv_ref[j + 1], (j + 1) % 3, 1 - bslot)

        new_expert = jnp.logical_or(j == 0, te_ref[j] != te_ref[jnp.maximum(j - 1, 0)])

        @pl.when(new_expert)
        def _():
            wgu_bf[...] = wgu_ref[...].astype(BF16)
            wd_bf[...] = wd_ref[...].astype(BF16)

        wait_gathers(nv, bslot)
        xg = xbuf[bslot].astype(BF16)
        gu = _dot(xg, wgu_bf[...])
        a = gu[:, :ff]
        u = gu[:, ff:]
        hidden = (a * (1.0 / (1.0 + jnp.exp(-a))) * u).astype(BF16)
        y = _dot(hidden, wd_bf[...])

        @pl.when(j >= 2)
        def _():
            wait_scatters(nv_ref[jnp.maximum(j - 2, 0)], bslot)

        ybuf[bslot] = y
        start_scatters(nv, mslot, bslot)

        @pl.when(j == nu - 1)
        def _():
            wait_scatters(nv, bslot)

            @pl.when(j >= 1)
            def _():
                wait_scatters(nv_ref[jnp.maximum(j - 1, 0)], 1 - bslot)


def _routed_experts(x1, meta, tile_expert, tile_valid, n_used, w_gu, w_down):
    t, d = x1.shape
    ff = w_down.shape[1]
    tm = EXPERT_TILE
    n_tiles = tile_expert.shape[0]
    tok_bits = max((t - 1).bit_length(), 1)
    kern = functools.partial(_expert_kernel, n_tok=t, tok_bits=tok_bits)
    grid_spec = pltpu.PrefetchScalarGridSpec(
        num_scalar_prefetch=3,
        grid=(n_tiles,),
        in_specs=[pl.BlockSpec(memory_space=pl.ANY),
                  pl.BlockSpec(memory_space=pl.ANY),
                  pl.BlockSpec((None, d, 2 * ff), lambda j, te, nv, nu: (te[j], 0, 0)),
                  pl.BlockSpec((None, ff, d), lambda j, te, nv, nu: (te[j], 0, 0))],
        out_specs=pl.BlockSpec(memory_space=pl.ANY),
        scratch_shapes=[pltpu.SMEM((3, tm), I32),
                        pltpu.VMEM((2, tm, d), F32),
                        pltpu.VMEM((2, tm, d), F32),
                        pltpu.VMEM((d, 2 * ff), BF16),
                        pltpu.VMEM((ff, d), BF16),
                        pltpu.SemaphoreType.DMA((3,)),
                        pltpu.SemaphoreType.DMA((2,)),
                        pltpu.SemaphoreType.DMA((2,))],
    )
    return pl.pallas_call(
        kern,
        grid_spec=grid_spec,
        out_shape=jax.ShapeDtypeStruct((TOP_K * t, d), F32),
        compiler_params=_params(("arbitrary",), 56),
        name="routed_experts",
    )(tile_expert, tile_valid, n_used, meta, x1, w_gu, w_down)


def _dispatch_plan(eid, rank, counts, t):
    tm = EXPERT_TILE
    n_tiles = (TOP_K * t) // tm + N_EXPERTS
    n_slots = n_tiles * tm
    tok_bits = max((t - 1).bit_length(), 1)
    padded = ((counts + tm - 1) // tm) * tm
    ends = jnp.cumsum(padded)
    offs = ends - padded
    n_used = (ends[-1] // tm).astype(I32)
    tiles = jnp.arange(n_tiles, dtype=I32)
    tile_expert = jnp.searchsorted(ends, jnp.minimum(tiles, n_used - 1) * tm, side="right").astype(I32)
    tile_expert = jnp.clip(tile_expert, 0, N_EXPERTS - 1)
    tile_valid = jnp.clip(counts[tile_expert] - (tiles * tm - offs[tile_expert]), 0, tm).astype(I32)
    slot = offs[eid] + rank
    tok = lax.broadcasted_iota(I32, (TOP_K, t), 1)
    k = lax.broadcasted_iota(I32, (TOP_K, t), 0)
    meta = jnp.zeros((n_slots,), I32)
    meta = meta.at[slot.reshape(-1)].set((tok | (k << tok_bits)).reshape(-1), unique_indices=True)
    return meta, tile_expert, tile_valid, n_used.reshape(1)


def _combine_kernel(x_ref, gate_ref, wgu_ref, wd_ref, g_ref, b_ref, *refs):
    y_refs = refs[:TOP_K]
    o_ref, ob_ref = refs[TOP_K:]
    ff = wd_ref.shape[0]
    x = x_ref[...]
    gu = _dot(x.astype(BF16), wgu_ref[...])
    a = gu[:, :ff]
    u = gu[:, ff:]
    hidden = (a * (1.0 / (1.0 + jnp.exp(-a))) * u).astype(BF16)
    ffn = _dot(hidden, wd_ref[...])
    gates = gate_ref[...]
    for k in range(TOP_K):
        ffn = ffn + gates[:, k:k + 1] * y_refs[k][...]
    out = _layer_norm(ALPHA * x + ffn, g_ref[...], b_ref[...])
    o_ref[...] = out
    ob_ref[...] = out.astype(BF16)


def _combine_norm(x1, gates_tk, y, w_gu_bf, w_down_bf, g, b):
    t, d = x1.shape
    ff = w_down_bf.shape[0]
    tm = min(128, t)
    nb = t // tm
    y_specs = [pl.BlockSpec((tm, d), functools.partial(lambda i, k: (k * nb + i, 0), k=k))
               for k in range(TOP_K)]
    return pl.pallas_call(
        _combine_kernel,
        grid=(nb,),
        in_specs=[pl.BlockSpec((tm, d), lambda i: (i, 0)),
                  pl.BlockSpec((tm, TOP_K), lambda i: (i, 0)),
                  pl.BlockSpec((d, 2 * ff), lambda i: (0, 0)),
                  pl.BlockSpec((ff, d), lambda i: (0, 0)),
                  pl.BlockSpec((1, d), lambda i: (0, 0)),
                  pl.BlockSpec((1, d), lambda i: (0, 0))] + y_specs,
        out_specs=[pl.BlockSpec((tm, d), lambda i: (i, 0)),
                   pl.BlockSpec((tm, d), lambda i: (i, 0))],
        out_shape=[jax.ShapeDtypeStruct((t, d), F32),
                   jax.ShapeDtypeStruct((t, d), BF16)],
        compiler_params=_params(("arbitrary",), 48),
        name="combine_norm",
    )(x1, gates_tk, w_gu_bf, w_down_bf, g.reshape(1, d), b.reshape(1, d), *([y] * TOP_K))


def _moe_block(x1, router, router_bias, exp_w_gu, exp_w_down, shared_w_gu, shared_w_down, g, b):
    t = x1.shape[0]
    eid, rank, gate, counts = _router(x1, router, router_bias)
    meta, tile_expert, tile_valid, n_used = _dispatch_plan(eid, rank, counts[:, 0], t)
    y = _routed_experts(x1, meta, tile_expert, tile_valid, n_used, exp_w_gu, exp_w_down)
    return _combine_norm(x1, gate.T, y, shared_w_gu.astype(BF16), shared_w_down.astype(BF16), g, b)


def _layer(x, xb, kv, mixer, w_in, forget_bias, w_o, ln_attn_g, ln_attn_b, router, router_bias,
           exp_w_gu, exp_w_down, shared_w_gu, shared_w_down, ln_ffn_g, ln_ffn_b):
    mix_w = MIX_HEADS * HEAD_DIM
    n_proj = 3 * mix_w + MEM_HEADS * HEAD_DIM
    proj = _in_proj(xb, w_in, n_proj)
    if mixer == 0:
        o_mix = _stick_attention(proj, MIX_HEADS)
    else:
        cum, cum_t = _forget_cumsum(x, w_in[:, n_proj:], forget_bias)
        o_mix = _fox_attention(proj, cum, cum_t, MIX_HEADS)
    o_mem = _memory_attention(proj, kv, 3 * mix_w)
    x1 = _out_proj_norm(o_mix, o_mem, w_o.astype(BF16), x, ln_attn_g, ln_attn_b)
    return _moe_block(x1, router, router_bias, exp_w_gu, exp_w_down, shared_w_gu, shared_w_down,
                      ln_ffn_g, ln_ffn_b)


def kernel(x, mem, mem_ln_g, mem_ln_b, w_mem_kv,
           l0_w_in, l0_w_o, l0_ln_attn_g, l0_ln_attn_b, l0_router, l0_router_bias,
           l0_exp_w_gu, l0_exp_w_down, l0_shared_w_gu, l0_shared_w_down, l0_ln_ffn_g, l0_ln_ffn_b,
           l1_w_in, l1_forget_bias, l1_w_o, l1_ln_attn_g, l1_ln_attn_b, l1_router, l1_router_bias,
           l1_exp_w_gu, l1_exp_w_down, l1_shared_w_gu, l1_shared_w_down, l1_ln_ffn_g, l1_ln_ffn_b):
    batch, seq, d = x.shape
    assert batch == 1
    kv = _memory_kv(mem[0], mem_ln_g, mem_ln_b, w_mem_kv)
    x2 = x[0]
    x2, xb = _layer(x2, x2.astype(BF16), kv, 0, l0_w_in, None, l0_w_o, l0_ln_attn_g, l0_ln_attn_b,
                    l0_router, l0_router_bias, l0_exp_w_gu, l0_exp_w_down, l0_shared_w_gu,
                    l0_shared_w_down, l0_ln_ffn_g, l0_ln_ffn_b)
    x2, _ = _layer(x2, xb, kv, 1, l1_w_in, l1_forget_bias, l1_w_o, l1_ln_attn_g, l1_ln_attn_b,
                   l1_router, l1_router_bias, l1_exp_w_gu, l1_exp_w_down, l1_shared_w_gu,
                   l1_shared_w_down, l1_ln_ffn_g, l1_ln_ffn_b)
    return x2.reshape(batch, seq, d)
```

```python
import functools

import jax
import jax.numpy as jnp
from jax import lax
from jax.experimental import pallas as pl
from jax.experimental.pallas import tpu as pltpu

F32 = jnp.float32
BF16 = jnp.bfloat16
I32 = jnp.int32

HEAD_DIM = 128
MIX_HEADS = 12
MEM_HEADS = 4
N_EXPERTS = 64
TOP_K = 8
N_GROUPS = 8
GROUP_SIZE = N_EXPERTS // N_GROUPS
TOPK_GROUPS = 4
ROUTED_SCALE = 2.5
DEPTH = 2
ALPHA = float((2 * DEPTH) ** 0.25)
LN_EPS = 1e-5

MIB = 1024 * 1024
ATT_TILE = 256
FOX_TILE = 512
EXPERT_TILE = 256
ROW_CHUNK = 8
EXP_ZERO = -104.0
NEG_INF = float("-inf")


def _params(semantics, vmem_mib):
    return pltpu.CompilerParams(dimension_semantics=semantics, vmem_limit_bytes=vmem_mib * MIB)


def _dot(a, b):
    return jnp.dot(a, b, preferred_element_type=F32)


def _dot_nt(a, b):
    return lax.dot_general(a, b, (((1,), (1,)), ((), ())), preferred_element_type=F32)


def _split2(x):
    hi = x.astype(BF16)
    lo = (x - hi.astype(F32)).astype(BF16)
    return hi, lo


def _split3(x):
    hi = x.astype(BF16)
    r = x - hi.astype(F32)
    mid = r.astype(BF16)
    lo = (r - mid.astype(F32)).astype(BF16)
    return hi, mid, lo


def _log_sigmoid(z):
    return jnp.minimum(z, 0.0) - jnp.log(1.0 + jnp.exp(-jnp.abs(z)))


def _layer_norm(y, g, b):
    mu = jnp.mean(y, axis=-1, keepdims=True)
    d = y - mu
    var = jnp.mean(d * d, axis=-1, keepdims=True)
    return d * lax.rsqrt(var + LN_EPS) * g + b


def _kv_kernel(mem_ref, g_ref, b_ref, w_ref, o_ref):
    y = _layer_norm(mem_ref[...], g_ref[...], b_ref[...])
    o_ref[...] = _dot(y.astype(BF16), w_ref[...].astype(BF16)).astype(o_ref.dtype)


def _memory_kv(mem, g, b, w):
    n, d = mem.shape
    nw = w.shape[1]
    tn = 512
    return pl.pallas_call(
        _kv_kernel,
        grid=(nw // tn,),
        in_specs=[pl.BlockSpec((n, d), lambda j: (0, 0)),
                  pl.BlockSpec((1, d), lambda j: (0, 0)),
                  pl.BlockSpec((1, d), lambda j: (0, 0)),
                  pl.BlockSpec((d, tn), lambda j: (0, j))],
        out_specs=pl.BlockSpec((n, tn), lambda j: (0, j)),
        out_shape=jax.ShapeDtypeStruct((n, nw), BF16),
        compiler_params=_params(("arbitrary",), 32),
        name="memory_kv",
    )(mem, g.reshape(1, d), b.reshape(1, d), w)


def _in_proj_kernel(x_ref, w_ref, o_ref, wbf_ref, *, k_col0, qmem_col0, scale):
    j = pl.program_id(0)
    tn = o_ref.shape[1]

    @pl.when(pl.program_id(1) == 0)
    def _():
        wbf_ref[...] = w_ref[...].astype(BF16)

    col0 = j * tn
    is_query = jnp.logical_or(col0 < k_col0, col0 >= qmem_col0)
    factor = jnp.where(is_query, scale, 1.0).astype(F32)
    o_ref[...] = (_dot(x_ref[...], wbf_ref[...]) * factor).astype(o_ref.dtype)


def _in_proj(xb, w, n_out, k_col0, qmem_col0):
    m, d = xb.shape
    tm, tn = 1024, 512
    tm = min(tm, m)
    assert k_col0 % tn == 0 and qmem_col0 % tn == 0
    kern = functools.partial(_in_proj_kernel, k_col0=k_col0, qmem_col0=qmem_col0,
                             scale=HEAD_DIM ** -0.5)
    return pl.pallas_call(
        kern,
        grid=(n_out // tn, m // tm),
        in_specs=[pl.BlockSpec((tm, d), lambda j, i: (i, 0)),
                  pl.BlockSpec((d, tn), lambda j, i: (0, j))],
        out_specs=pl.BlockSpec((tm, tn), lambda j, i: (i, j)),
        out_shape=jax.ShapeDtypeStruct((m, n_out), BF16),
        scratch_shapes=[pltpu.VMEM((d, tn), BF16)],
        compiler_params=_params(("arbitrary", "arbitrary"), 40),
        name="in_proj",
    )(xb, w)


def _gate_kernel(x_ref, w_ref, b_ref, cum_ref, cum_t_ref, carry_ref):
    tm = x_ref.shape[0]

    @pl.when(pl.program_id(0) == 0)
    def _():
        carry_ref[...] = jnp.zeros_like(carry_ref)

    xh, xl = _split2(x_ref[...])
    wh, wl = _split2(w_ref[...])
    f = _dot(xh, wh) + _dot(xh, wl) + _dot(xl, wh) + b_ref[...]
    lf = _log_sigmoid(f)
    row = lax.broadcasted_iota(I32, (tm, tm), 0)
    col = lax.broadcasted_iota(I32, (tm, tm), 1)
    lower = (col <= row).astype(BF16)
    p0, p1, p2 = _split3(lf)
    cum = _dot(lower, p0) + _dot(lower, p1) + _dot(lower, p2) + carry_ref[...]
    carry_ref[...] = cum[tm - 1:tm, :]
    cum_ref[...] = cum
    cum_t_ref[...] = cum.T[:cum_t_ref.shape[0], :]


def _forget_cumsum(x, w_f, bias):
    t, d = x.shape
    h = w_f.shape[1]
    tm = min(256, t)
    w_pad = jnp.zeros((d, HEAD_DIM), F32).at[:, :h].set(w_f)
    b_pad = jnp.zeros((1, HEAD_DIM), F32).at[0, :h].set(bias)
    return pl.pallas_call(
        _gate_kernel,
        grid=(t // tm,),
        in_specs=[pl.BlockSpec((tm, d), lambda i: (i, 0)),
                  pl.BlockSpec((d, HEAD_DIM), lambda i: (0, 0)),
                  pl.BlockSpec((1, HEAD_DIM), lambda i: (0, 0))],
        out_specs=[pl.BlockSpec((tm, HEAD_DIM), lambda i: (i, 0)),
                   pl.BlockSpec((16, tm), lambda i: (0, i))],
        out_shape=[jax.ShapeDtypeStruct((t, HEAD_DIM), F32),
                   jax.ShapeDtypeStruct((16, t), F32)],
        scratch_shapes=[pltpu.VMEM((1, HEAD_DIM), F32)],
        compiler_params=_params(("arbitrary",), 32),
        name="forget_cumsum",
    )(x, w_pad, b_pad)


def _stick_kernel(q_ref, k_ref, v_ref, o_ref):
    tq = q_ref.shape[0]
    i = pl.program_id(1)
    q = q_ref[...]
    row = lax.broadcasted_iota(I32, (tq, tq), 0)
    col = lax.broadcasted_iota(I32, (tq, tq), 1)
    strict = col < row
    later = (row > col).astype(BF16)

    def block(j, c, acc, masked):
        start = pl.multiple_of(j * tq, tq)
        kb = k_ref[pl.ds(start, tq), :]
        vb = v_ref[pl.ds(start, tq), :]
        z = _dot_nt(q, kb)
        soft =jnp.log(1.0 + jnp.exp(-jnp.abs(z)))
        log_beta = jnp.minimum(z, 0.0) - soft
        log_stay = jnp.minimum(-z, 0.0) - soft
        if masked:
            log_stay = jnp.where(strict, log_stay, 0.0)
        hi, lo = _split2(log_stay)
        log_after = _dot(hi, later) + _dot(lo, later) + c
        w = jnp.exp(log_beta + log_after)
        if masked:
            w = jnp.where(strict, w, 0.0)
        acc = acc + _dot(w.astype(BF16), vb)
        c = c + jnp.sum(log_stay, axis=1, keepdims=True)
        return c, acc

    c0 = jnp.zeros((tq, 1), F32)
    acc0 = jnp.zeros((tq, HEAD_DIM), F32)
    c1, acc1 = block(i, c0, acc0, True)

    def cond(state):
        j, alive, _, _ = state
        return jnp.logical_and(j >= 0, alive > 0)

    def body(state):
        j, _, c, acc = state
        c, acc = block(j, c, acc, False)
        alive = (jnp.max(c) > EXP_ZERO).astype(I32)
        return j - 1, alive, c, acc

    alive1 = (jnp.max(c1) > EXP_ZERO).astype(I32)
    _, _, _, acc = lax.while_loop(cond, body, (i - 1, alive1, c1, acc1))
    o_ref[...] = acc.astype(o_ref.dtype)


def _stick_attention(proj, n_heads):
    t = proj.shape[0]
    tq = min(ATT_TILE, t)
    return pl.pallas_call(
        _stick_kernel,
        grid=(n_heads, t // tq),
        in_specs=[pl.BlockSpec((tq, HEAD_DIM), lambda h, i: (i, h)),
                  pl.BlockSpec((t, HEAD_DIM), lambda h, i: (0, n_heads + h)),
                  pl.BlockSpec((t, HEAD_DIM), lambda h, i: (0, 2 * n_heads + h))],
        out_specs=pl.BlockSpec((tq, HEAD_DIM), lambda h, i: (i, h)),
        out_shape=jax.ShapeDtypeStruct((t, n_heads * HEAD_DIM), BF16),
        compiler_params=_params(("arbitrary", "arbitrary"), 40),
        name="stick_attention",
    )(proj, proj, proj)


def _fox_kernel(q_ref, k_ref, v_ref, fq_ref, fk_ref, o_ref):
    tq = q_ref.shape[0]
    h = pl.program_id(0)
    i = pl.program_id(1)
    q = q_ref[...]
    lane = lax.broadcasted_iota(I32, fq_ref.shape, 1)
    fq = jnp.sum(jnp.where(lane == h, fq_ref[...], 0.0), axis=1, keepdims=True)
    row = lax.broadcasted_iota(I32, (tq, tq), 0)
    col = lax.broadcasted_iota(I32, (tq, tq), 1)
    causal = col <= row

    def block(j, m, l, acc, masked):
        start = pl.multiple_of(j * tq, tq)
        kb = k_ref[pl.ds(start, tq), :]
        vb = v_ref[pl.ds(start, tq), :]
        fk = fk_ref[0, j]
        s = _dot_nt(q, kb) + (fq - fk)
        if masked:
            s = jnp.where(causal, s, NEG_INF)
        m_new = jnp.maximum(m, jnp.max(s, axis=1, keepdims=True))
        alpha = jnp.exp(m - m_new)
        p = jnp.exp(s - m_new)
        l = alpha * l + jnp.sum(p, axis=1, keepdims=True)
        acc = alpha * acc + _dot(p.astype(BF16), vb)
        return m_new, l, acc

    m0 = jnp.full((tq, 1), NEG_INF, F32)
    l0 = jnp.zeros((tq, 1), F32)
    acc0 = jnp.zeros((tq, HEAD_DIM), F32)
    state = block(i, m0, l0, acc0, True)

    def body(n, state):
        return block(i - 1 - n, *state, False)

    _, l, acc = lax.fori_loop(0, i, body, state)
    o_ref[...] = (acc * (1.0 / l)).astype(o_ref.dtype)


def _fox_attention(proj, cum, cum_t, n_heads):
    t = proj.shape[0]
    tq = min(FOX_TILE, t)
    fk = cum_t.reshape(cum_t.shape[0], t // tq, 1, tq)
    return pl.pallas_call(
        _fox_kernel,
        grid=(n_heads, t // tq),
        in_specs=[pl.BlockSpec((tq, HEAD_DIM), lambda h, i: (i, h)),
                  pl.BlockSpec((t, HEAD_DIM), lambda h, i: (0, n_heads + h)),
                  pl.BlockSpec((t, HEAD_DIM), lambda h, i: (0, 2 * n_heads + h)),
                  pl.BlockSpec((tq, HEAD_DIM), lambda h, i: (i, 0)),
                  pl.BlockSpec((1, t // tq, 1, tq), lambda h, i: (h, 0, 0, 0))],
        out_specs=pl.BlockSpec((tq, HEAD_DIM), lambda h, i: (i, h)),
        out_shape=jax.ShapeDtypeStruct((t, n_heads * HEAD_DIM), BF16),
        compiler_params=_params(("arbitrary", "arbitrary"), 40),
        name="fox_attention",
    )(proj, proj, proj, cum, fk)


def _mem_attn_kernel(q_ref, k_ref, v_ref, o_ref):
    s = _dot_nt(q_ref[...], k_ref[...])
    m = jnp.max(s, axis=1, keepdims=True)
    p = jnp.exp(s - m)
    l = jnp.sum(p, axis=1, keepdims=True)
    o_ref[...] = (_dot(p.astype(BF16), v_ref[...]) * (1.0 / l)).astype(o_ref.dtype)


def _memory_attention(proj, kv, q_col0):
    t = proj.shape[0]
    n_mem = kv.shape[0]
    tq = min(1024, t)
    c0 = q_col0 // HEAD_DIM
    return pl.pallas_call(
        _mem_attn_kernel,
        grid=(MEM_HEADS, t // tq),
        in_specs=[pl.BlockSpec((tq, HEAD_DIM), lambda h, i: (i, c0 + h)),
                  pl.BlockSpec((n_mem, HEAD_DIM), lambda h, i: (0, h)),
                  pl.BlockSpec((n_mem, HEAD_DIM), lambda h, i: (0, MEM_HEADS + h))],
        out_specs=pl.BlockSpec((tq, HEAD_DIM), lambda h, i: (i, h)),
        out_shape=jax.ShapeDtypeStruct((t, MEM_HEADS * HEAD_DIM), BF16),
        compiler_params=_params(("arbitrary", "arbitrary"), 32),
        name="memory_attention",
    )(proj, kv, kv)


def _out_proj_kernel(om_ref, oc_ref, w_ref, x_ref, g_ref, b_ref, y_ref):
    n_mix = om_ref.shape[1]
    mix = _dot(om_ref[...], w_ref[:n_mix, :]) + _dot(oc_ref[...], w_ref[n_mix:, :])
    y_ref[...] = _layer_norm(ALPHA * x_ref[...] + mix, g_ref[...], b_ref[...])


def _out_proj_norm(o_mix, o_mem, w_o_bf, x, g, b):
    t, d = x.shape
    tm = min(512, t)
    n_mix, n_mem = o_mix.shape[1], o_mem.shape[1]
    return pl.pallas_call(
        _out_proj_kernel,
        grid=(t // tm,),
        in_specs=[pl.BlockSpec((tm, n_mix), lambda i: (i, 0)),
                  pl.BlockSpec((tm, n_mem), lambda i: (i, 0)),
                  pl.BlockSpec((n_mix + n_mem, d), lambda i: (0, 0)),
                  pl.BlockSpec((tm, d), lambda i: (i, 0)),
                  pl.BlockSpec((1, d), lambda i: (0, 0)),
                  pl.BlockSpec((1, d), lambda i: (0, 0))],
        out_specs=pl.BlockSpec((tm, d), lambda i: (i, 0)),
        out_shape=jax.ShapeDtypeStruct((t, d), F32),
        compiler_params=_params(("arbitrary",), 48),
        name="out_proj_norm",
    )(o_mix, o_mem, w_o_bf, x, g.reshape(1, d), b.reshape(1, d))


def _router_kernel(x_ref, wh_ref, wl_ref, b_ref, eid_ref, rank_ref, gate_ref, cnt_ref, carry_ref):
    tm = x_ref.shape[0]

    @pl.when(pl.program_id(0) == 0)
    def _():
        carry_ref[...] = jnp.zeros_like(carry_ref)

    xh, xl = _split2(x_ref[...])
    wh, wl = wh_ref[...], wl_ref[...]
    logits = _dot_nt(wh, xh) + _dot_nt(wh, xl) + _dot_nt(wl, xh)
    scores = 1.0 / (1.0 + jnp.exp(-logits))
    sel = scores + b_ref[:, 0:1]

    group_score = []
    for g in range(N_GROUPS):
        v = sel[g * GROUP_SIZE:(g + 1) * GROUP_SIZE, :]
        m1 = jnp.max(v, axis=0, keepdims=True)
        is_max = v == m1
        n_max = jnp.sum(is_max.astype(F32), axis=0, keepdims=True)
        m2 = jnp.max(jnp.where(is_max, NEG_INF, v), axis=0, keepdims=True)
        group_score.append(m1 + jnp.where(n_max >= 2.0, m1, m2))
    masked = []
    for g in range(N_GROUPS):
        ahead = jnp.zeros((1, tm), F32)
        for g2 in range(N_GROUPS):
            if g2 == g:
                continue
            beats = group_score[g2] > group_score[g]
            if g2 < g:
                beats = jnp.logical_or(beats, group_score[g2] == group_score[g])
            ahead = ahead + beats.astype(F32)
        v = sel[g * GROUP_SIZE:(g + 1) * GROUP_SIZE, :]
        masked.append(jnp.where(ahead < float(TOPK_GROUPS), v, NEG_INF))
    msel = jnp.concatenate(masked, axis=0)

    e_idx = lax.broadcasted_iota(I32, (N_EXPERTS, tm), 0)
    ahead = jnp.zeros((N_EXPERTS, tm), F32)
    for e2 in range(N_EXPERTS):
        other = msel[e2:e2 + 1, :]
        beats = jnp.logical_or(other > msel, jnp.logical_and(other == msel, e_idx > e2))
        ahead = ahead + beats.astype(F32)
    chosen = ahead < float(TOP_K)
    chosen_f = chosen.astype(F32)

    w = jnp.where(chosen, scores, 0.0)
    gates = w / jnp.sum(w, axis=0, keepdims=True) * ROUTED_SCALE

    chosen_b = chosen_f.astype(BF16)
    r64 = lax.broadcasted_iota(I32, (N_EXPERTS, N_EXPERTS), 0)
    c64 = lax.broadcasted_iota(I32, (N_EXPERTS, N_EXPERTS), 1)
    choice = _dot((c64 < r64).astype(BF16), chosen_b)
    rt = lax.broadcasted_iota(I32, (tm, tm), 0)
    ct = lax.broadcasted_iota(I32, (tm, tm), 1)
    rank = _dot(chosen_b, (rt < ct).astype(BF16)) + carry_ref[:, 0:1]
    carry_ref[...] = carry_ref[...] + jnp.sum(chosen_f, axis=1, keepdims=True)
    cnt_ref[...] = carry_ref[...].astype(I32)

    e_f = e_idx.astype(F32)
    eids, ranks, gts = [], [], []
    for k in range(TOP_K):
        pick = jnp.logical_and(chosen, choice == float(k))
        eids.append(jnp.sum(jnp.where(pick, e_f, 0.0), axis=0, keepdims=True))
        ranks.append(jnp.sum(jnp.where(pick, rank, 0.0), axis=0, keepdims=True))
        gts.append(jnp.sum(jnp.where(pick, gates, 0.0), axis=0, keepdims=True))
    eid_ref[...] = jnp.concatenate(eids, axis=0).astype(I32)
    rank_ref[...] = jnp.concatenate(ranks, axis=0).astype(I32)
    gate_ref[...] = jnp.concatenate(gts, axis=0)


def _router(x1, w_router, router_bias):
    t, d = x1.shape
    tm = min(512, t)
    wt = w_router.T
    wh, wl = _split2(wt)
    bias = jnp.broadcast_to(router_bias.astype(F32)[:, None], (N_EXPERTS, HEAD_DIM))
    return pl.pallas_call(
        _router_kernel,
        grid=(t // tm,),
        in_specs=[pl.BlockSpec((tm, d), lambda i: (i, 0)),
                  pl.BlockSpec((N_EXPERTS, d), lambda i: (0, 0)),
                  pl.BlockSpec((N_EXPERTS, d), lambda i: (0, 0)),
                  pl.BlockSpec((N_EXPERTS, HEAD_DIM), lambda i: (0, 0))],
        out_specs=[pl.BlockSpec((TOP_K, tm), lambda i: (0, i)),
                   pl.BlockSpec((TOP_K, tm), lambda i: (0, i)),
                   pl.BlockSpec((TOP_K, tm), lambda i: (0, i)),
                   pl.BlockSpec((N_EXPERTS, HEAD_DIM), lambda i: (0, 0))],
        out_shape=[jax.ShapeDtypeStruct((TOP_K, t), I32),
                   jax.ShapeDtypeStruct((TOP_K, t), I32),
                   jax.ShapeDtypeStruct((TOP_K, t), F32),
                   jax.ShapeDtypeStruct((N_EXPERTS, HEAD_DIM), I32)],
        scratch_shapes=[pltpu.VMEM((N_EXPERTS, HEAD_DIM), F32)],
        compiler_params=_params(("arbitrary",), 40),
        name="router",
    )(x1, wh, wl, bias)


def _expert_kernel(te_ref, nv_ref, nu_ref, meta_hbm, x_hbm, wgu_ref, wd_ref, y_hbm,
                   meta_s, xbuf, ybuf, wgu_bf, wd_bf, msem, gsem, ssem, *, n_tok, tok_bits):
    tm = xbuf.shape[1]
    ff = wd_ref.shape[0]
    j = pl.program_id(0)
    nu = nu_ref[0]
    tok_mask = (1 << tok_bits) - 1

    def meta_copy(tile, slot):
        return pltpu.make_async_copy(meta_hbm.at[pl.ds(tile * tm, tm)],
                                     meta_s.at[pl.ds(slot * tm, tm)], msem.at[slot])

    def n_chunks(nv):
        return (nv + ROW_CHUNK - 1) // ROW_CHUNK

    def start_gathers(nv, mslot, bslot):
        def chunk(c, _):
            for u in range(ROW_CHUNK):
                r = c * ROW_CHUNK + u
                tok = meta_s[mslot * tm + r] & tok_mask
                pltpu.make_async_copy(x_hbm.at[pl.ds(tok, 1), :], xbuf.at[bslot, pl.ds(r, 1), :],
                                      gsem.at[bslot]).start()
            return 0
        lax.fori_loop(0, n_chunks(nv), chunk, 0)

    def start_scatters(nv, mslot, bslot):
        def chunk(c, _):
            for u in range(ROW_CHUNK):
                r = c * ROW_CHUNK + u
                meta = meta_s[mslot * tm + r]
                real = (meta >> tok_bits) * n_tok + (meta & tok_mask)
                spare = TOP_K * n_tok + bslot * ROW_CHUNK + u
                dest = jnp.where(r < nv, real, spare)
                pltpu.make_async_copy(ybuf.at[bslot, pl.ds(r, 1), :], y_hbm.at[pl.ds(dest, 1), :],
                                      ssem.at[bslot]).start()
            return 0
        lax.fori_loop(0, n_chunks(nv), chunk, 0)

    def wait_gathers(nv, bslot):
        n = pl.multiple_of(n_chunks(nv) * ROW_CHUNK, ROW_CHUNK)
        pltpu.make_async_copy(x_hbm.at[pl.ds(0, n), :], xbuf.at[bslot, pl.ds(0, n), :],
                              gsem.at[bslot]).wait()

    def wait_scatters(nv, bslot):
        n = pl.multiple_of(n_chunks(nv) * ROW_CHUNK, ROW_CHUNK)
        pltpu.make_async_copy(ybuf.at[bslot, pl.ds(0, n), :], y_hbm.at[pl.ds(0, n), :],
                              ssem.at[bslot]).wait()

    @pl.when(j == 0)
    def _():
        xbuf[...] = jnp.zeros_like(xbuf)
        ybuf[...] = jnp.zeros_like(ybuf)
        spare_init = pltpu.make_async_copy(
            ybuf.at[0, pl.ds(0, 2 * ROW_CHUNK), :],
            y_hbm.at[pl.ds(TOP_K * n_tok, 2 * ROW_CHUNK), :], ssem.at[0])
        spare_init.start()
        spare_init.wait()
        meta_copy(0, 0).start()
        meta_copy(0, 0).wait()
        start_gathers(nv_ref[0], 0, 0)

        @pl.when(nu > 1)
        def _():
            meta_copy(1, 1).start()

    @pl.when(j < nu)
    def _():
        bslot = j % 2
        mslot = j % 3
        nv = nv_ref[j]

        @pl.when(j + 2 < nu)
        def _():
            meta_copy(j + 2, (j + 2) % 3).start()

        @pl.when(j + 1 < nu)
        def _():
            meta_copy(j + 1, (j + 1) % 3).wait()
            start_gathers(nv_ref[j + 1], (j + 1) % 3, 1 - bslot)

        new_expert = jnp.logical_or(j == 0, te_ref[j] != te_ref[jnp.maximum(j - 1, 0)])

        @pl.when(new_expert)
        def _():
            wgu_bf[...] = wgu_ref[...].astype(BF16)
            wd_bf[...] = wd_ref[...].astype(BF16)

        wait_gathers(nv, bslot)
        xg = xbuf[bslot].astype(BF16)
        gu = _dot(xg, wgu_bf[...])
        a = gu[:, :ff]
        u = gu[:, ff:]
        hidden = (a * (1.0 / (1.0 + jnp.exp(-a))) * u).astype(BF16)
        y = _dot(hidden, wd_bf[...])

        @pl.when(j >= 2)
        def _():
            wait_scatters(nv_ref[jnp.maximum(j - 2, 0)], bslot)

        ybuf[bslot] = y
        start_scatters(nv, mslot, bslot)

        @pl.when(j == nu - 1)
        def _():
            wait_scatters(nv, bslot)

            @pl.when(j >= 1)
            def _():
                wait_scatters(nv_ref[jnp.maximum(j - 1, 0)], 1 - bslot)


def _routed_experts(x1, meta, tile_expert, tile_valid, n_used, w_gu, w_down):
    t, d = x1.shape
    ff = w_down.shape[1]
    tm = EXPERT_TILE
    n_tiles = tile_expert.shape[0]
    tok_bits = max((t - 1).bit_length(), 1)
    kern = functools.partial(_expert_kernel, n_tok=t, tok_bits=tok_bits)
    grid_spec = pltpu.PrefetchScalarGridSpec(
        num_scalar_prefetch=3,
        grid=(n_tiles,),
        in_specs=[pl.BlockSpec(memory_space=pl.ANY),
                  pl.BlockSpec(memory_space=pl.ANY),
                  pl.BlockSpec((None, d, 2 * ff), lambda j, te, nv, nu: (te[j], 0, 0)),
                  pl.BlockSpec((None, ff, d), lambda j, te, nv, nu: (te[j], 0, 0))],
        out_specs=pl.BlockSpec(memory_space=pl.ANY),
        scratch_shapes=[pltpu.SMEM((3 * tm,), I32),
                        pltpu.VMEM((2, tm, d), F32),
                        pltpu.VMEM((2, tm, d), F32),
                        pltpu.VMEM((d, 2 * ff), BF16),
                        pltpu.VMEM((ff, d), BF16),
                        pltpu.SemaphoreType.DMA((3,)),
                        pltpu.SemaphoreType.DMA((2,)),
                        pltpu.SemaphoreType.DMA((2,))],
    )
    return pl.pallas_call(
        kern,
        grid_spec=grid_spec,
        out_shape=jax.ShapeDtypeStruct((TOP_K * t + 2 * ROW_CHUNK, d), F32),
        compiler_params=_params(("arbitrary",), 56),
        name="routed_experts",
    )(tile_expert, tile_valid, n_used, meta, x1, w_gu, w_down)


def _dispatch_plan(eid, rank, counts, t):
    tm = EXPERT_TILE
    n_tiles = (TOP_K * t) // tm + N_EXPERTS
    n_slots = n_tiles * tm
    tok_bits = max((t - 1).bit_length(), 1)
    padded = ((counts + tm - 1) // tm) * tm
    ends = jnp.cumsum(padded)
    offs = ends - padded
    n_used = (ends[-1] // tm).astype(I32)
    tiles = jnp.arange(n_tiles, dtype=I32)
    experts = jnp.arange(N_EXPERTS, dtype=I32)
    tile_start = jnp.minimum(tiles, n_used - 1) * tm
    tile_expert = jnp.sum((ends[None, :] <= tile_start[:, None]).astype(I32), axis=1)
    tile_expert = jnp.clip(tile_expert, 0, N_EXPERTS - 1)
    of_tile = tile_expert[:, None] == experts[None, :]
    tile_count = jnp.sum(jnp.where(of_tile, counts[None, :], 0), axis=1)
    tile_offs = jnp.sum(jnp.where(of_tile, offs[None, :], 0), axis=1)
    tile_valid = jnp.clip(tile_count - (tiles * tm - tile_offs), 0, tm).astype(I32)
    slot = rank + jnp.sum(jnp.where(eid[:, :, None] == experts, offs, 0), axis=2)
    tok = lax.broadcasted_iota(I32, (TOP_K, t), 1)
    k = lax.broadcasted_iota(I32, (TOP_K, t), 0)
    meta = jnp.zeros((n_slots,), I32)
    meta = meta.at[slot.reshape(-1)].set((tok | (k << tok_bits)).reshape(-1), unique_indices=True)
    return meta, tile_expert, tile_valid, n_used.reshape(1)


def _combine_kernel(x_ref, gate_ref, wgu_ref, wd_ref, g_ref, b_ref, *refs):
    y_refs = refs[:TOP_K]
    o_ref, ob_ref = refs[TOP_K:]
    ff = wd_ref.shape[0]
    x = x_ref[...]
    gu = _dot(x.astype(BF16), wgu_ref[...])
    a = gu[:, :ff]
    u = gu[:, ff:]
    hidden = (a * (1.0 / (1.0 + jnp.exp(-a))) * u).astype(BF16)
    ffn = _dot(hidden, wd_ref[...])
    gates = gate_ref[...]
    for k in range(TOP_K):
        ffn = ffn + gates[:, k:k + 1] * y_refs[k][...]
    out = _layer_norm(ALPHA * x + ffn, g_ref[...], b_ref[...])
    o_ref[...] = out
    ob_ref[...] = out.astype(BF16)


def _combine_norm(x1, gates_tk, y, w_gu_bf, w_down_bf, g, b):
    t, d = x1.shape
    ff = w_down_bf.shape[0]
    tm = min(128, t)
    nb = t // tm
    y_specs = [pl.BlockSpec((tm, d), functools.partial(lambda i, k: (k * nb + i, 0), k=k))
               for k in range(TOP_K)]
    return pl.pallas_call(
        _combine_kernel,
        grid=(nb,),
        in_specs=[pl.BlockSpec((tm, d), lambda i: (i, 0)),
                  pl.BlockSpec((tm, TOP_K), lambda i: (i, 0)),
                  pl.BlockSpec((d, 2 * ff), lambda i: (0, 0)),
                  pl.BlockSpec((ff, d), lambda i: (0, 0)),
                  pl.BlockSpec((1, d), lambda i: (0, 0)),
                  pl.BlockSpec((1, d), lambda i: (0, 0))] + y_specs,
        out_specs=[pl.BlockSpec((tm, d), lambda i: (i, 0)),
                   pl.BlockSpec((tm, d), lambda i: (i, 0))],
        out_shape=[jax.ShapeDtypeStruct((t, d), F32),
                   jax.ShapeDtypeStruct((t, d), BF16)],
        compiler_params=_params(("arbitrary",), 48),
        name="combine_norm",
    )(x1, gates_tk, w_gu_bf, w_down_bf, g.reshape(1, d), b.reshape(1, d), *([y] * TOP_K))


def _moe_block(x1, router, router_bias, exp_w_gu, exp_w_down, shared_w_gu, shared_w_down, g, b):
    t = x1.shape[0]
    eid, rank, gate, counts = _router(x1, router, router_bias)
    meta, tile_expert, tile_valid, n_used = _dispatch_plan(eid, rank, counts[:, 0], t)
    y = _routed_experts(x1, meta, tile_expert, tile_valid, n_used, exp_w_gu, exp_w_down)
    return _combine_norm(x1, gate.T, y, shared_w_gu.astype(BF16), shared_w_down.astype(BF16), g, b)


def _layer(x, xb, kv, mixer, w_in, forget_bias, w_o, ln_attn_g, ln_attn_b, router, router_bias,
           exp_w_gu, exp_w_down, shared_w_gu, shared_w_down, ln_ffn_g, ln_ffn_b):
    mix_w = MIX_HEADS * HEAD_DIM
    n_proj = 3 * mix_w + MEM_HEADS * HEAD_DIM
    proj = _in_proj(xb, w_in, n_proj, mix_w, 3 * mix_w)
    if mixer == 0:
        o_mix = _stick_attention(proj, MIX_HEADS)
    else:
        cum, cum_t = _forget_cumsum(x, w_in[:, n_proj:], forget_bias)
        o_mix = _fox_attention(proj, cum, cum_t, MIX_HEADS)
    o_mem = _memory_attention(proj, kv, 3 * mix_w)
    x1 = _out_proj_norm(o_mix, o_mem, w_o.astype(BF16), x, ln_attn_g, ln_attn_b)
    return _moe_block(x1, router, router_bias, exp_w_gu, exp_w_down, shared_w_gu, shared_w_down,
                      ln_ffn_g, ln_ffn_b)


def kernel(x, mem, mem_ln_g, mem_ln_b, w_mem_kv,
           l0_w_in, l0_w_o, l0_ln_attn_g, l0_ln_attn_b, l0_router, l0_router_bias,
           l0_exp_w_gu, l0_exp_w_down, l0_shared_w_gu, l0_shared_w_down, l0_ln_ffn_g, l0_ln_ffn_b,
           l1_w_in, l1_forget_bias, l1_w_o, l1_ln_attn_g, l1_ln_attn_b, l1_router, l1_router_bias,
           l1_exp_w_gu, l1_exp_w_down, l1_shared_w_gu, l1_shared_w_down, l1_ln_ffn_g, l1_ln_ffn_b):
    batch, seq, d = x.shape
    assert batch == 1
    kv = _memory_kv(mem[0], mem_ln_g, mem_ln_b, w_mem_kv)
    x2 = x[0]
    x2, xb = _layer(x2, x2.astype(BF16), kv, 0, l0_w_in, None, l0_w_o, l0_ln_attn_g, l0_ln_attn_b,
                    l0_router, l0_router_bias, l0_exp_w_gu, l0_exp_w_down, l0_shared_w_gu,
                    l0_shared_w_down, l0_ln_ffn_g, l0_ln_ffn_b)
    x2, _ = _layer(x2, xb, kv, 1, l1_w_in, l1_forget_bias, l1_w_o, l1_ln_attn_g, l1_ln_attn_b,
                   l1_router, l1_router_bias, l1_exp_w_gu, l1_exp_w_down, l1_shared_w_gu,
                   l1_shared_w_down, l1_ln_ffn_g, l1_ln_ffn_b)
    return x2.reshape(batch, seq, d)
```

```python
import functools

import jax
import jax.numpy as jnp
from jax import lax
from jax.experimental import pallas as pl
from jax.experimental.pallas import tpu as pltpu

F32 = jnp.float32
BF16 = jnp.bfloat16
I32 = jnp.int32

HEAD_DIM = 128
MIX_HEADS = 12
MEM_HEADS = 4
N_EXPERTS = 64
TOP_K = 8
N_GROUPS = 8
GROUP_SIZE = N_EXPERTS // N_GROUPS
TOPK_GROUPS = 4
ROUTED_SCALE = 2.5
DEPTH = 2
ALPHA = float((2 * DEPTH) ** 0.25)
LN_EPS = 1e-5

MIB = 1024 * 1024
ATT_TILE = 256
FOX_TILE = 512
EXPERT_TILE = 256
META_RING = 4
EXP_ZERO = -104.0
NEG_INF = float("-inf")


def _params(semantics, vmem_mib):
    return pltpu.CompilerParams(dimension_semantics=semantics, vmem_limit_bytes=vmem_mib * MIB)


def _dot(a, b):
    return jnp.dot(a, b, preferred_element_type=F32)


def _dot_nt(a, b):
    return lax.dot_general(a, b, (((1,), (1,)), ((), ())), preferred_element_type=F32)


def _split2(x):
    hi = x.astype(BF16)
    lo = (x - hi.astype(F32)).astype(BF16)
    return hi, lo


def _split3(x):
    hi = x.astype(BF16)
    r = x - hi.astype(F32)
    mid = r.astype(BF16)
    lo = (r - mid.astype(F32)).astype(BF16)
    return hi, mid, lo


def _log_sigmoid(z):
    return jnp.minimum(z, 0.0) - jnp.log(1.0 + jnp.exp(-jnp.abs(z)))


def _layer_norm(y, g, b):
    mu = jnp.mean(y, axis=-1, keepdims=True)
    d = y - mu
    var = jnp.mean(d * d, axis=-1, keepdims=True)
    return d * lax.rsqrt(var + LN_EPS) * g + b


def _kv_kernel(mem_ref, g_ref, b_ref, w_ref, o_ref):
    y = _layer_norm(mem_ref[...], g_ref[...], b_ref[...])
    o_ref[...] = _dot(y.astype(BF16), w_ref[...].astype(BF16)).astype(o_ref.dtype)


def _memory_kv(mem, g, b, w):
    n, d = mem.shape
    nw = w.shape[1]
    tn = 512
    return pl.pallas_call(
        _kv_kernel,
        grid=(nw // tn,),
        in_specs=[pl.BlockSpec((n, d), lambda j: (0, 0)),
                  pl.BlockSpec((1, d), lambda j: (0, 0)),
                  pl.BlockSpec((1, d), lambda j: (0, 0)),
                  pl.BlockSpec((d, tn), lambda j: (0, j))],
        out_specs=pl.BlockSpec((n, tn), lambda j: (0, j)),
        out_shape=jax.ShapeDtypeStruct((n, nw), BF16),
        compiler_params=_params(("arbitrary",), 32),
        name="memory_kv",
    )(mem, g.reshape(1, d), b.reshape(1, d), w)


def _in_proj_kernel(x_ref, w_ref, o_ref, wbf_ref, *, k_col0, qmem_col0, scale):
    j = pl.program_id(0)
    tn = o_ref.shape[1]

    @pl.when(pl.program_id(1) == 0)
    def _():
        wbf_ref[...] = w_ref[...].astype(BF16)

    col0 = j * tn
    is_query = jnp.logical_or(col0 < k_col0, col0 >= qmem_col0)
    factor = jnp.where(is_query, scale, 1.0).astype(F32)
    o_ref[...] = (_dot(x_ref[...], wbf_ref[...]) * factor).astype(o_ref.dtype)


def _in_proj(xb, w, n_out, k_col0, qmem_col0):
    m, d = xb.shape
    tm, tn = 1024, 512
    tm = min(tm, m)
    assert k_col0 % tn == 0 and qmem_col0 % tn == 0
    kern = functools.partial(_in_proj_kernel, k_col0=k_col0, qmem_col0=qmem_col0,
                             scale=HEAD_DIM ** -0.5)
    return pl.pallas_call(
        kern,
        grid=(n_out // tn, m // tm),
        in_specs=[pl.BlockSpec((tm, d), lambda j, i: (i, 0)),
                  pl.BlockSpec((d, tn), lambda j, i: (0, j))],
        out_specs=pl.BlockSpec((tm, tn), lambda j, i: (i, j)),
        out_shape=jax.ShapeDtypeStruct((m, n_out), BF16),
        scratch_shapes=[pltpu.VMEM((d, tn), BF16)],
        compiler_params=_params(("arbitrary", "arbitrary"), 40),
        name="in_proj",
    )(xb, w)


def _gate_kernel(x_ref, w_ref, b_ref, cum_ref, cum_t_ref, carry_ref):
    tm = x_ref.shape[0]

    @pl.when(pl.program_id(0) == 0)
    def _():
        carry_ref[...] = jnp.zeros_like(carry_ref)

    xh, xl = _split2(x_ref[...])
    wh, wl = _split2(w_ref[...])
    f = _dot(xh, wh) + _dot(xh, wl) + _dot(xl, wh) + b_ref[...]
    lf = _log_sigmoid(f)
    row = lax.broadcasted_iota(I32, (tm, tm), 0)
    col = lax.broadcasted_iota(I32, (tm, tm), 1)
    lower = (col <= row).astype(BF16)
    p0, p1, p2 = _split3(lf)
    cum = _dot(lower, p0) + _dot(lower, p1) + _dot(lower, p2) + carry_ref[...]
    carry_ref[...] = cum[tm - 1:tm, :]
    cum_ref[...] = cum
    cum_t_ref[...] = cum.T[:cum_t_ref.shape[0], :]


def _forget_cumsum(x, w_f, bias):
    t, d = x.shape
    h = w_f.shape[1]
    tm = min(256, t)
    w_pad = jnp.zeros((d, HEAD_DIM), F32).at[:, :h].set(w_f)
    b_pad = jnp.zeros((1, HEAD_DIM), F32).at[0, :h].set(bias)
    return pl.pallas_call(
        _gate_kernel,
        grid=(t // tm,),
        in_specs=[pl.BlockSpec((tm, d), lambda i: (i, 0)),
                  pl.BlockSpec((d, HEAD_DIM), lambda i: (0, 0)),
                  pl.BlockSpec((1, HEAD_DIM), lambda i: (0, 0))],
        out_specs=[pl.BlockSpec((tm, HEAD_DIM), lambda i: (i, 0)),
                   pl.BlockSpec((16, tm), lambda i: (0, i))],
        out_shape=[jax.ShapeDtypeStruct((t, HEAD_DIM), F32),
                   jax.ShapeDtypeStruct((16, t), F32)],
        scratch_shapes=[pltpu.VMEM((1, HEAD_DIM), F32)],
        compiler_params=_params(("arbitrary",), 32),
        name="forget_cumsum",
    )(x, w_pad, b_pad)


def _stick_kernel(q_ref, k_ref, v_ref, o_ref):
    tq = q_ref.shape[0]
    i = pl.program_id(1)
    q = q_ref[...]
    row = lax.broadcasted_iota(I32, (tq, tq), 0)
    col = lax.broadcasted_iota(I32, (tq, tq), 1)
    strict = col < row
    later = (row > col).astype(BF16)

    def block(j, c, acc, masked):
        start = pl.multiple_of(j * tq, tq)
        kb = k_ref[pl.ds(start, tq), :]
        vb = v_ref[pl.ds(start, tq), :]
        z = _dot_nt(q, kb)
        soft =jnp.log(1.0 + jnp.exp(-jnp.abs(z)))
        log_beta = jnp.minimum(z, 0.0) - soft
        log_stay = jnp.minimum(-z, 0.0) - soft
        if masked:
            log_stay = jnp.where(strict, log_stay, 0.0)
        hi, lo = _split2(log_stay)
        log_after = _dot(hi, later) + _dot(lo, later) + c
        w = jnp.exp(log_beta + log_after)
        if masked:
            w = jnp.where(strict, w, 0.0)
        acc = acc + _dot(w.astype(BF16), vb)
        c = c + jnp.sum(log_stay, axis=1, keepdims=True)
        return c, acc

    c0 = jnp.zeros((tq, 1), F32)
    acc0 = jnp.zeros((tq, HEAD_DIM), F32)
    c1, acc1 = block(i, c0, acc0, True)

    def cond(state):
        j, alive, _, _ = state
        return jnp.logical_and(j >= 0, alive > 0)

    def body(state):
        j, _, c, acc = state
        c, acc = block(j, c, acc, False)
        alive = (jnp.max(c) > EXP_ZERO).astype(I32)
        return j - 1, alive, c, acc

    alive1 = (jnp.max(c1) > EXP_ZERO).astype(I32)
    _, _, _, acc = lax.while_loop(cond, body, (i - 1, alive1, c1, acc1))
    o_ref[...] = acc.astype(o_ref.dtype)


def _stick_attention(proj, n_heads):
    t = proj.shape[0]
    tq = min(ATT_TILE, t)
    return pl.pallas_call(
        _stick_kernel,
        grid=(n_heads, t // tq),
        in_specs=[pl.BlockSpec((tq, HEAD_DIM), lambda h, i: (i, h)),
                  pl.BlockSpec((t, HEAD_DIM), lambda h, i: (0, n_heads + h)),
                  pl.BlockSpec((t, HEAD_DIM), lambda h, i: (0, 2 * n_heads + h))],
        out_specs=pl.BlockSpec((tq, HEAD_DIM), lambda h, i: (i, h)),
        out_shape=jax.ShapeDtypeStruct((t, n_heads * HEAD_DIM), BF16),
        compiler_params=_params(("arbitrary", "arbitrary"), 40),
        name="stick_attention",
    )(proj, proj, proj)


def _fox_kernel(q_ref, k_ref, v_ref, fq_ref, fk_ref, o_ref):
    tq = q_ref.shape[0]
    h = pl.program_id(0)
    i = pl.program_id(1)
    q = q_ref[...]
    lane = lax.broadcasted_iota(I32, fq_ref.shape, 1)
    fq = jnp.sum(jnp.where(lane == h, fq_ref[...], 0.0), axis=1, keepdims=True)
    row = lax.broadcasted_iota(I32, (tq, tq), 0)
    col = lax.broadcasted_iota(I32, (tq, tq), 1)
    causal = col <= row

    def block(j, m, l, acc, masked):
        start = pl.multiple_of(j * tq, tq)
        kb = k_ref[pl.ds(start, tq), :]
        vb = v_ref[pl.ds(start, tq), :]
        fk = fk_ref[0, j]
        s = _dot_nt(q, kb) + (fq - fk)
        if masked:
            s = jnp.where(causal, s, NEG_INF)
        m_new = jnp.maximum(m, jnp.max(s, axis=1, keepdims=True))
        alpha = jnp.exp(m - m_new)
        p = jnp.exp(s - m_new)
        l = alpha * l + jnp.sum(p, axis=1, keepdims=True)
        acc = alpha * acc + _dot(p.astype(BF16), vb)
        return m_new, l, acc

    m0 = jnp.full((tq, 1), NEG_INF, F32)
    l0 = jnp.zeros((tq, 1), F32)
    acc0 = jnp.zeros((tq, HEAD_DIM), F32)
    state = block(i, m0, l0, acc0, True)

    def body(n, state):
        return block(i - 1 - n, *state, False)

    _, l, acc = lax.fori_loop(0, i, body, state)
    o_ref[...] = (acc * (1.0 / l)).astype(o_ref.dtype)


def _fox_attention(proj, cum, cum_t, n_heads):
    t = proj.shape[0]
    tq = min(FOX_TILE, t)
    fk = cum_t.reshape(cum_t.shape[0], t // tq, 1, tq)
    return pl.pallas_call(
        _fox_kernel,
        grid=(n_heads, t // tq),
        in_specs=[pl.BlockSpec((tq, HEAD_DIM), lambda h, i: (i, h)),
                  pl.BlockSpec((t, HEAD_DIM), lambda h, i: (0, n_heads + h)),
                  pl.BlockSpec((t, HEAD_DIM), lambda h, i: (0, 2 * n_heads + h)),
                  pl.BlockSpec((tq, HEAD_DIM), lambda h, i: (i, 0)),
                  pl.BlockSpec((1, t // tq, 1, tq), lambda h, i: (h, 0, 0, 0))],
        out_specs=pl.BlockSpec((tq, HEAD_DIM), lambda h, i: (i, h)),
        out_shape=jax.ShapeDtypeStruct((t, n_heads * HEAD_DIM), BF16),
        compiler_params=_params(("arbitrary", "arbitrary"), 40),
        name="fox_attention",
    )(proj, proj, proj, cum, fk)


def _mem_attn_kernel(q_ref, k_ref, v_ref, o_ref):
    s = _dot_nt(q_ref[...], k_ref[...])
    m = jnp.max(s, axis=1, keepdims=True)
    p = jnp.exp(s - m)
    l = jnp.sum(p, axis=1, keepdims=True)
    o_ref[...] = (_dot(p.astype(BF16), v_ref[...]) * (1.0 / l)).astype(o_ref.dtype)


def _memory_attention(proj, kv, q_col0):
    t = proj.shape[0]
    n_mem = kv.shape[0]
    tq = min(1024, t)
    c0 = q_col0 // HEAD_DIM
    return pl.pallas_call(
        _mem_attn_kernel,
        grid=(MEM_HEADS, t // tq),
        in_specs=[pl.BlockSpec((tq, HEAD_DIM), lambda h, i: (i, c0 + h)),
                  pl.BlockSpec((n_mem, HEAD_DIM), lambda h, i: (0, h)),
                  pl.BlockSpec((n_mem, HEAD_DIM), lambda h, i: (0, MEM_HEADS + h))],
        out_specs=pl.BlockSpec((tq, HEAD_DIM), lambda h, i: (i, h)),
        out_shape=jax.ShapeDtypeStruct((t, MEM_HEADS * HEAD_DIM), BF16),
        compiler_params=_params(("arbitrary", "arbitrary"), 32),
        name="memory_attention",
    )(proj, kv, kv)


def _out_proj_kernel(om_ref, oc_ref, w_ref, x_ref, g_ref, b_ref, y_ref):
    n_mix = om_ref.shape[1]
    mix = _dot(om_ref[...], w_ref[:n_mix, :]) + _dot(oc_ref[...], w_ref[n_mix:, :])
    y_ref[...] = _layer_norm(ALPHA * x_ref[...] + mix, g_ref[...], b_ref[...])


def _out_proj_norm(o_mix, o_mem, w_o_bf, x, g, b):
    t, d = x.shape
    tm = min(512, t)
    n_mix, n_mem = o_mix.shape[1], o_mem.shape[1]
    return pl.pallas_call(
        _out_proj_kernel,
        grid=(t // tm,),
        in_specs=[pl.BlockSpec((tm, n_mix), lambda i: (i, 0)),
                  pl.BlockSpec((tm, n_mem), lambda i: (i, 0)),
                  pl.BlockSpec((n_mix + n_mem, d), lambda i: (0, 0)),
                  pl.BlockSpec((tm, d), lambda i: (i, 0)),
                  pl.BlockSpec((1, d), lambda i: (0, 0)),
                  pl.BlockSpec((1, d), lambda i: (0, 0))],
        out_specs=pl.BlockSpec((tm, d), lambda i: (i, 0)),
        out_shape=jax.ShapeDtypeStruct((t, d), F32),
        compiler_params=_params(("arbitrary",), 48),
        name="out_proj_norm",
    )(o_mix, o_mem, w_o_bf, x, g.reshape(1, d), b.reshape(1, d))


def _router_kernel(x_ref, wh_ref, wl_ref, b_ref, eid_ref, rank_ref, gate_ref, cnt_ref, carry_ref):
    tm = x_ref.shape[0]

    @pl.when(pl.program_id(0) == 0)
    def _():
        carry_ref[...] = jnp.zeros_like(carry_ref)

    xh, xl = _split2(x_ref[...])
    wh, wl = wh_ref[...], wl_ref[...]
    logits = _dot_nt(wh, xh) + _dot_nt(wh, xl) + _dot_nt(wl, xh)
    scores = 1.0 / (1.0 + jnp.exp(-logits))
    sel = scores + b_ref[:, 0:1]

    group_score = []
    for g in range(N_GROUPS):
        v = sel[g * GROUP_SIZE:(g + 1) * GROUP_SIZE, :]
        m1 = jnp.max(v, axis=0, keepdims=True)
        is_max = v == m1
        n_max = jnp.sum(is_max.astype(F32), axis=0, keepdims=True)
        m2 = jnp.max(jnp.where(is_max, NEG_INF, v), axis=0, keepdims=True)
        group_score.append(m1 + jnp.where(n_max >= 2.0, m1, m2))
    masked = []
    for g in range(N_GROUPS):
        ahead = jnp.zeros((1, tm), F32)
        for g2 in range(N_GROUPS):
            if g2 == g:
                continue
            beats = group_score[g2] > group_score[g]
            if g2 < g:
                beats = jnp.logical_or(beats, group_score[g2] == group_score[g])
            ahead = ahead + beats.astype(F32)
        v = sel[g * GROUP_SIZE:(g + 1) * GROUP_SIZE, :]
        masked.append(jnp.where(ahead < float(TOPK_GROUPS), v, NEG_INF))
    msel = jnp.concatenate(masked, axis=0)

    e_idx = lax.broadcasted_iota(I32, (N_EXPERTS, tm), 0)
    ahead = jnp.zeros((N_EXPERTS, tm), F32)
    for e2 in range(N_EXPERTS):
        other = msel[e2:e2 + 1, :]
        beats = jnp.logical_or(other > msel, jnp.logical_and(other == msel, e_idx > e2))
        ahead = ahead + beats.astype(F32)
    chosen = ahead < float(TOP_K)
    chosen_f = chosen.astype(F32)

    w = jnp.where(chosen, scores, 0.0)
    gates = w / jnp.sum(w, axis=0, keepdims=True) * ROUTED_SCALE

    chosen_b = chosen_f.astype(BF16)
    r64 = lax.broadcasted_iota(I32, (N_EXPERTS, N_EXPERTS), 0)
    c64 = lax.broadcasted_iota(I32, (N_EXPERTS, N_EXPERTS), 1)
    choice = _dot((c64 < r64).astype(BF16), chosen_b)
    rt = lax.broadcasted_iota(I32, (tm, tm), 0)
    ct = lax.broadcasted_iota(I32, (tm, tm), 1)
    rank = _dot(chosen_b, (rt < ct).astype(BF16)) + carry_ref[:, 0:1]
    carry_ref[...] = carry_ref[...] + jnp.sum(chosen_f, axis=1, keepdims=True)
    cnt_ref[...] = carry_ref[...].astype(I32)

    e_f = e_idx.astype(F32)
    eids, ranks, gts = [], [], []
    for k in range(TOP_K):
        pick = jnp.logical_and(chosen, choice == float(k))
        eids.append(jnp.sum(jnp.where(pick, e_f, 0.0), axis=0, keepdims=True))
        ranks.append(jnp.sum(jnp.where(pick, rank, 0.0), axis=0, keepdims=True))
        gts.append(jnp.sum(jnp.where(pick, gates, 0.0), axis=0, keepdims=True))
    eid_ref[...] = jnp.concatenate(eids, axis=0).astype(I32)
    rank_ref[...] = jnp.concatenate(ranks, axis=0).astype(I32)
    gate_ref[...] = jnp.concatenate(gts, axis=0)


def _router(x1, w_router, router_bias):
    t, d = x1.shape
    tm = min(512, t)
    wt = w_router.T
    wh, wl = _split2(wt)
    bias = jnp.broadcast_to(router_bias.astype(F32)[:, None], (N_EXPERTS, HEAD_DIM))
    return pl.pallas_call(
        _router_kernel,
        grid=(t // tm,),
        in_specs=[pl.BlockSpec((tm, d), lambda i: (i, 0)),
                  pl.BlockSpec((N_EXPERTS, d), lambda i: (0, 0)),
                  pl.BlockSpec((N_EXPERTS, d), lambda i: (0, 0)),
                  pl.BlockSpec((N_EXPERTS, HEAD_DIM), lambda i: (0, 0))],
        out_specs=[pl.BlockSpec((TOP_K, tm), lambda i: (0, i)),
                   pl.BlockSpec((TOP_K, tm), lambda i: (0, i)),
                   pl.BlockSpec((TOP_K, tm), lambda i: (0, i)),
                   pl.BlockSpec((N_EXPERTS, HEAD_DIM), lambda i: (0, 0))],
        out_shape=[jax.ShapeDtypeStruct((TOP_K, t), I32),
                   jax.ShapeDtypeStruct((TOP_K, t), I32),
                   jax.ShapeDtypeStruct((TOP_K, t), F32),
                   jax.ShapeDtypeStruct((N_EXPERTS, HEAD_DIM), I32)],
        scratch_shapes=[pltpu.VMEM((N_EXPERTS, HEAD_DIM), F32)],
        compiler_params=_params(("arbitrary",), 40),
        name="router",
    )(x1, wh, wl, bias)


def _expert_kernel(te_ref, nu_ref, meta_hbm, x_hbm, wgu_ref, wd_ref, y_hbm,
                   meta_s, xa, xb, ya, yb, wgu_bf, wd_bf, msem, gsem, ssem, *, n_tok):
    tm = xa.shape[0]
    ff = wd_ref.shape[0]
    j = pl.program_id(0)
    nu = nu_ref[0]

    def meta_copy(a):
        slot = a % META_RING
        return pltpu.make_async_copy(meta_hbm.at[pl.ds(a * 2 * tm, 2 * tm)],
                                     meta_s.at[pl.ds(slot * 2 * tm, 2 * tm)], msem.at[slot])

    def start_gathers(a, x_dst, sem):
        base = (a % META_RING) * 2 * tm
        for r in range(tm):
            tok = meta_s[base + r]
            pltpu.make_async_copy(x_hbm.at[pl.ds(tok, 1), :], x_dst.at[pl.ds(r, 1), :], sem).start()

    def start_scatters(a, y_src, sem):
        base = (a % META_RING) * 2 * tm + tm
        for r in range(tm):
            dest = meta_s[base + r]
            pltpu.make_async_copy(y_src.at[pl.ds(r, 1), :], y_hbm.at[pl.ds(dest, 1), :], sem).start()

    def wait_gathers(x_dst, sem):
        pltpu.make_async_copy(x_hbm.at[pl.ds(0, tm), :], x_dst, sem).wait()

    def wait_scatters(y_src, sem):
        pltpu.make_async_copy(y_src, y_hbm.at[pl.ds(0, tm), :], sem).wait()

    @pl.when(j == 0)
    def _():
        ya[...] = jnp.zeros_like(ya)
        yb[...] = jnp.zeros_like(yb)
        pltpu.make_async_copy(ya, y_hbm.at[pl.ds(TOP_K * n_tok, tm), :], ssem.at[0]).start()
        meta_copy(0).start()
        meta_copy(1).start()
        meta_copy(2).start()
        meta_copy(0).wait()
        meta_copy(1).wait()
        start_gathers(1, xa, gsem.at[0])

    def step(x_cur, y_cur, x_nxt, y_prv, p):
        wait_gathers(x_cur, gsem.at[p])
        wait_scatters(y_cur, ssem.at[p])
        start_gathers(j + 2, x_nxt, gsem.at[1 - p])
        start_scatters(j, y_prv, ssem.at[1 - p])
        gu = _dot(x_cur[...].astype(BF16), wgu_bf[...])
        a = gu[:, :ff]
        u = gu[:, ff:]
        hidden = (a * (1.0 / (1.0 + jnp.exp(-a))) * u).astype(BF16)
        y_cur[...] = _dot(hidden, wd_bf[...])

        @pl.when(j == nu - 1)
        def _():
            start_scatters(j + 1, y_cur, ssem.at[p])
            wait_scatters(y_prv, ssem.at[1 - p])
            wait_scatters(y_cur, ssem.at[p])
            wait_gathers(x_nxt, gsem.at[1 - p])
            meta_copy(j + 3).wait()

    @pl.when(j < nu)
    def _():
        meta_copy(j + 3).start()
        meta_copy(j + 2).wait()
        new_expert = jnp.logical_or(j == 0, te_ref[j] != te_ref[jnp.maximum(j - 1, 0)])

        @pl.when(new_expert)
        def _():
            wgu_bf[...] = wgu_ref[...].astype(BF16)
            wd_bf[...] = wd_ref[...].astype(BF16)

        @pl.when(j % 2 == 0)
        def _():
            step(xa, ya, xb, yb, 0)

        @pl.when(j % 2 == 1)
        def _():
            step(xb, yb, xa, ya, 1)


def _routed_experts(x1, meta, tile_expert, n_used, w_gu, w_down):
    t, d = x1.shape
    ff = w_down.shape[1]
    tm = EXPERT_TILE
    n_tiles = tile_expert.shape[0]
    kern = functools.partial(_expert_kernel, n_tok=t)
    grid_spec = pltpu.PrefetchScalarGridSpec(
        num_scalar_prefetch=2,
        grid=(n_tiles,),
        in_specs=[pl.BlockSpec(memory_space=pl.ANY),
                  pl.BlockSpec(memory_space=pl.ANY),
                  pl.BlockSpec((None, d, 2 * ff), lambda j, te, nu: (te[j], 0, 0)),
                  pl.BlockSpec((None, ff, d), lambda j, te, nu: (te[j], 0, 0))],
        out_specs=pl.BlockSpec(memory_space=pl.ANY),
        scratch_shapes=[pltpu.SMEM((META_RING * 2 * tm,), I32),
                        pltpu.VMEM((tm, d), F32),
                        pltpu.VMEM((tm, d), F32),
                        pltpu.VMEM((tm, d), F32),
                        pltpu.VMEM((tm, d), F32),
                        pltpu.VMEM((d, 2 * ff), BF16),
                        pltpu.VMEM((ff, d), BF16),
                        pltpu.SemaphoreType.DMA((META_RING,)),
                        pltpu.SemaphoreType.DMA((2,)),
                        pltpu.SemaphoreType.DMA((2,))],
    )
    return pl.pallas_call(
        kern,
        grid_spec=grid_spec,
        out_shape=jax.ShapeDtypeStruct((TOP_K * t + 2 * tm, d), F32),
        compiler_params=_params(("arbitrary",), 56),
        name="routed_experts",
    )(tile_expert, n_used, meta, x1, w_gu, w_down)


def _dispatch_plan(eid, rank, counts, t):
    tm = EXPERT_TILE
    n_tiles = (TOP_K * t) // tm + N_EXPERTS
    n_slots = n_tiles * tm
    tok_bits = max((t - 1).bit_length(), 1)
    padded = ((counts + tm - 1) // tm) * tm
    ends = jnp.cumsum(padded)
    offs = ends - padded
    n_used = (ends[-1] // tm).astype(I32)
    tiles = jnp.arange(n_tiles, dtype=I32)
    experts = jnp.arange(N_EXPERTS, dtype=I32)
    tile_start = jnp.minimum(tiles, n_used - 1) * tm
    tile_expert = jnp.sum((ends[None, :] <= tile_start[:, None]).astype(I32), axis=1)
    tile_expert = jnp.clip(tile_expert, 0, N_EXPERTS - 1)
    of_tile = tile_expert[:, None] == experts[None, :]
    tile_count = jnp.sum(jnp.where(of_tile, counts[None, :], 0), axis=1)
    tile_offs = jnp.sum(jnp.where(of_tile, offs[None, :], 0), axis=1)
    tile_valid = jnp.clip(tile_count - (tiles * tm - tile_offs), 0, tm).astype(I32)
    slot = rank + jnp.sum(jnp.where(eid[:, :, None] == experts, offs, 0), axis=2)
    tok = lax.broadcasted_iota(I32, (TOP_K, t), 1)
    k = lax.broadcasted_iota(I32, (TOP_K, t), 0)
    packed = jnp.zeros((n_slots,), I32)
    packed = packed.at[slot.reshape(-1)].set((tok | (k << tok_bits)).reshape(-1), unique_indices=True)
    packed = packed.reshape(n_tiles, tm)
    row = lax.broadcasted_iota(I32, (n_tiles, tm), 1)
    gather_rows = packed & ((1 << tok_bits) - 1)
    real = (packed >> tok_bits) * t + gather_rows
    spare = TOP_K * t + (tiles[:, None] % 2) * tm + row
    scatter_rows = jnp.where(row < tile_valid[:, None], real, spare)
    plan = jnp.stack([gather_rows, scatter_rows], axis=1)
    dummy = jnp.stack([jnp.zeros((1, tm), I32), TOP_K * t + tm + row[:1]], axis=1)
    tail = jnp.concatenate([dummy, dummy], axis=0)
    meta = jnp.concatenate([dummy, plan, tail], axis=0).reshape(-1)
    return meta, tile_expert, n_used.reshape(1)


def _combine_kernel(x_ref, gate_ref, wgu_ref, wd_ref, g_ref, b_ref, *refs):
    y_refs = refs[:TOP_K]
    o_ref, ob_ref = refs[TOP_K:]
    ff = wd_ref.shape[0]
    x = x_ref[...]
    gu = _dot(x.astype(BF16), wgu_ref[...])
    a = gu[:, :ff]
    u = gu[:, ff:]
    hidden = (a * (1.0 / (1.0 + jnp.exp(-a))) * u).astype(BF16)
    ffn = _dot(hidden, wd_ref[...])
    gates = gate_ref[...]
    for k in range(TOP_K):
        ffn = ffn + gates[:, k:k + 1] * y_refs[k][...]
    out = _layer_norm(ALPHA * x + ffn, g_ref[...], b_ref[...])
    o_ref[...] = out
    ob_ref[...] = out.astype(BF16)


def _combine_norm(x1, gates_tk, y, w_gu_bf, w_down_bf, g, b):
    t, d = x1.shape
    ff = w_down_bf.shape[0]
    tm = min(128, t)
    nb = t // tm
    y_specs = [pl.BlockSpec((tm, d), functools.partial(lambda i, k: (k * nb + i, 0), k=k))
               for k in range(TOP_K)]
    return pl.pallas_call(
        _combine_kernel,
        grid=(nb,),
        in_specs=[pl.BlockSpec((tm, d), lambda i: (i, 0)),
                  pl.BlockSpec((tm, TOP_K), lambda i: (i, 0)),
                  pl.BlockSpec((d, 2 * ff), lambda i: (0, 0)),
                  pl.BlockSpec((ff, d), lambda i: (0, 0)),
                  pl.BlockSpec((1, d), lambda i: (0, 0)),
                  pl.BlockSpec((1, d), lambda i: (0, 0))] + y_specs,
        out_specs=[pl.BlockSpec((tm, d), lambda i: (i, 0)),
                   pl.BlockSpec((tm, d), lambda i: (i, 0))],
        out_shape=[jax.ShapeDtypeStruct((t, d), F32),
                   jax.ShapeDtypeStruct((t, d), BF16)],
        compiler_params=_params(("arbitrary",), 48),
        name="combine_norm",
    )(x1, gates_tk, w_gu_bf, w_down_bf, g.reshape(1, d), b.reshape(1, d), *([y] * TOP_K))


def _moe_block(x1, router, router_bias, exp_w_gu, exp_w_down, shared_w_gu, shared_w_down, g, b):
    t = x1.shape[0]
    eid, rank, gate, counts = _router(x1, router, router_bias)
    meta, tile_expert, n_used = _dispatch_plan(eid, rank, counts[:, 0], t)
    y = _routed_experts(x1, meta, tile_expert, n_used, exp_w_gu, exp_w_down)
    return _combine_norm(x1, gate.T, y, shared_w_gu.astype(BF16), shared_w_down.astype(BF16), g, b)


def _layer(x, xb, kv, mixer, w_in, forget_bias, w_o, ln_attn_g, ln_attn_b, router, router_bias,
           exp_w_gu, exp_w_down, shared_w_gu, shared_w_down, ln_ffn_g, ln_ffn_b):
    mix_w = MIX_HEADS * HEAD_DIM
    n_proj = 3 * mix_w + MEM_HEADS * HEAD_DIM
    proj = _in_proj(xb, w_in, n_proj, mix_w, 3 * mix_w)
    if mixer == 0:
        o_mix = _stick_attention(proj, MIX_HEADS)
    else:
        cum, cum_t = _forget_cumsum(x, w_in[:, n_proj:], forget_bias)
        o_mix = _fox_attention(proj, cum, cum_t, MIX_HEADS)
    o_mem = _memory_attention(proj, kv, 3 * mix_w)
    x1 = _out_proj_norm(o_mix, o_mem, w_o.astype(BF16), x, ln_attn_g, ln_attn_b)
    return _moe_block(x1, router, router_bias, exp_w_gu, exp_w_down, shared_w_gu, shared_w_down,
                      ln_ffn_g, ln_ffn_b)


def kernel(x, mem, mem_ln_g, mem_ln_b, w_mem_kv,
           l0_w_in, l0_w_o, l0_ln_attn_g, l0_ln_attn_b, l0_router, l0_router_bias,
           l0_exp_w_gu, l0_exp_w_down, l0_shared_w_gu, l0_shared_w_down, l0_ln_ffn_g, l0_ln_ffn_b,
           l1_w_in, l1_forget_bias, l1_w_o, l1_ln_attn_g, l1_ln_attn_b, l1_router, l1_router_bias,
           l1_exp_w_gu, l1_exp_w_down, l1_shared_w_gu, l1_shared_w_down, l1_ln_ffn_g, l1_ln_ffn_b):
    batch, seq, d = x.shape
    assert batch == 1
    kv = _memory_kv(mem[0], mem_ln_g, mem_ln_b, w_mem_kv)
    x2 = x[0]
    x2, xb = _layer(x2, x2.astype(BF16), kv, 0, l0_w_in, None, l0_w_o, l0_ln_attn_g, l0_ln_attn_b,
                    l0_router, l0_router_bias, l0_exp_w_gu, l0_exp_w_down, l0_shared_w_gu,
                    l0_shared_w_down, l0_ln_ffn_g, l0_ln_ffn_b)
    x2, _ = _layer(x2, xb, kv, 1, l1_w_in, l1_forget_bias, l1_w_o, l1_ln_attn_g, l1_ln_attn_b,
                   l1_router, l1_router_bias, l1_exp_w_gu, l1_exp_w_down, l1_shared_w_gu,
                   l1_shared_w_down, l1_ln_ffn_g, l1_ln_ffn_b)
    return x2.reshape(batch, seq, d)
```

```python
import functools

import jax
import jax.numpy as jnp
from jax import lax
from jax.experimental import pallas as pl
from jax.experimental.pallas import tpu as pltpu

F32 = jnp.float32
BF16 = jnp.bfloat16
I32 = jnp.int32
U32 = jnp.uint32

HEAD_DIM = 128
MIX_HEADS = 12
MEM_HEADS = 4
N_EXPERTS = 64
TOP_K = 8
N_GROUPS = 8
GROUP_SIZE = N_EXPERTS // N_GROUPS
TOPK_GROUPS = 4
ROUTED_SCALE = 2.5
DEPTH = 2
ALPHA = float((2 * DEPTH) ** 0.25)
LN_EPS = 1e-5

MIB = 1024 * 1024
ATT_TILE = 256
FOX_TILE = 512
EXPERT_TILE = 256
META_RING = 4
ROW_TILE = 8
EXP_ZERO = -104.0
NEG_INF = float("-inf")


def _params(semantics, vmem_mib):
    return pltpu.CompilerParams(dimension_semantics=semantics, vmem_limit_bytes=vmem_mib * MIB)


def _dot(a, b):
    return jnp.dot(a, b, preferred_element_type=F32)


def _dot_nt(a, b):
    return lax.dot_general(a, b, (((1,), (1,)), ((), ())), preferred_element_type=F32)


def _split2(x):
    hi = x.astype(BF16)
    lo = (x - hi.astype(F32)).astype(BF16)
    return hi, lo


def _split3(x):
    hi = x.astype(BF16)
    r = x - hi.astype(F32)
    mid = r.astype(BF16)
    lo = (r - mid.astype(F32)).astype(BF16)
    return hi, mid, lo


def _pack_rows(y):
    half = y.shape[1] // 2
    bits = lax.bitcast_convert_type(y.astype(BF16).astype(F32), U32)
    return (bits[:, half:] & U32(0xFFFF0000)) | (bits[:, :half] >> 16)


def _unpack_rows(words):
    lo = lax.bitcast_convert_type(words << 16, F32)
    hi = lax.bitcast_convert_type(words & U32(0xFFFF0000), F32)
    return jnp.concatenate([lo, hi], axis=1)


def _store_packed_rows(ref, y):
    rows = y.shape[0]
    words = _pack_rows(y)
    for c in range(ROW_TILE):
        ref[pl.ds(c, rows, stride=ROW_TILE), :] = words[:, c * HEAD_DIM:(c + 1) * HEAD_DIM]


def _load_packed_rows(ref):
    rows = ref.shape[0] // ROW_TILE
    words = [ref[pl.ds(c, rows, stride=ROW_TILE), :] for c in range(ROW_TILE)]
    return _unpack_rows(jnp.concatenate(words, axis=1))


def _log_sigmoid(z):
    return jnp.minimum(z, 0.0) - jnp.log(1.0 + jnp.exp(-jnp.abs(z)))


def _layer_norm(y, g, b):
    mu = jnp.mean(y, axis=-1, keepdims=True)
    d = y - mu
    var = jnp.mean(d * d, axis=-1, keepdims=True)
    return d * lax.rsqrt(var + LN_EPS) * g + b


def _kv_kernel(mem_ref, g_ref, b_ref, w_ref, o_ref):
    y = _layer_norm(mem_ref[...], g_ref[...], b_ref[...])
    o_ref[...] = _dot(y.astype(BF16), w_ref[...].astype(BF16)).astype(o_ref.dtype)


def _memory_kv(mem, g, b, w):
    n, d = mem.shape
    nw = w.shape[1]
    tn = 512
    return pl.pallas_call(
        _kv_kernel,
        grid=(nw // tn,),
        in_specs=[pl.BlockSpec((n, d), lambda j: (0, 0)),
                  pl.BlockSpec((1, d), lambda j: (0, 0)),
                  pl.BlockSpec((1, d), lambda j: (0, 0)),
                  pl.BlockSpec((d, tn), lambda j: (0, j))],
        out_specs=pl.BlockSpec((n, tn), lambda j: (0, j)),
        out_shape=jax.ShapeDtypeStruct((n, nw), BF16),
        compiler_params=_params(("arbitrary",), 32),
        name="memory_kv",
    )(mem, g.reshape(1, d), b.reshape(1, d), w)


def _in_proj_kernel(x_ref, w_ref, o_ref, wbf_ref, *, k_col0, qmem_col0, scale):
    j = pl.program_id(0)
    tn = o_ref.shape[1]

    @pl.when(pl.program_id(1) == 0)
    def _():
        wbf_ref[...] = w_ref[...].astype(BF16)

    col0 = j * tn
    is_query = jnp.logical_or(col0 < k_col0, col0 >= qmem_col0)
    factor = jnp.where(is_query, scale, 1.0).astype(F32)
    o_ref[...] = (_dot(x_ref[...], wbf_ref[...]) * factor).astype(o_ref.dtype)


def _in_proj(xb, w, n_out, k_col0, qmem_col0):
    m, d = xb.shape
    tm, tn = 1024, 512
    tm = min(tm, m)
    assert k_col0 % tn == 0 and qmem_col0 % tn == 0
    kern = functools.partial(_in_proj_kernel, k_col0=k_col0, qmem_col0=qmem_col0,
                             scale=HEAD_DIM ** -0.5)
    return pl.pallas_call(
        kern,
        grid=(n_out // tn, m // tm),
        in_specs=[pl.BlockSpec((tm, d), lambda j, i: (i, 0)),
                  pl.BlockSpec((d, tn), lambda j, i: (0, j))],
        out_specs=pl.BlockSpec((tm, tn), lambda j, i: (i, j)),
        out_shape=jax.ShapeDtypeStruct((m, n_out), BF16),
        scratch_shapes=[pltpu.VMEM((d, tn), BF16)],
        compiler_params=_params(("arbitrary", "arbitrary"), 40),
        name="in_proj",
    )(xb, w)


def _gate_kernel(x_ref, w_ref, b_ref, cum_ref, cum_t_ref, carry_ref):
    tm = x_ref.shape[0]

    @pl.when(pl.program_id(0) == 0)
    def _():
        carry_ref[...] = jnp.zeros_like(carry_ref)

    xh, xl = _split2(x_ref[...])
    wh, wl = _split2(w_ref[...])
    f = _dot(xh, wh) + _dot(xh, wl) + _dot(xl, wh) + b_ref[...]
    lf = _log_sigmoid(f)
    row = lax.broadcasted_iota(I32, (tm, tm), 0)
    col = lax.broadcasted_iota(I32, (tm, tm), 1)
    lower = (col <= row).astype(BF16)
    p0, p1, p2 = _split3(lf)
    cum = _dot(lower, p0) + _dot(lower, p1) + _dot(lower, p2) + carry_ref[...]
    carry_ref[...] = cum[tm - 1:tm, :]
    cum_ref[...] = cum
    cum_t_ref[...] = cum.T[:cum_t_ref.shape[0], :]


def _forget_cumsum(x, w_f, bias):
    t, d = x.shape
    h = w_f.shape[1]
    tm = min(256, t)
    w_pad = jnp.zeros((d, HEAD_DIM), F32).at[:, :h].set(w_f)
    b_pad = jnp.zeros((1, HEAD_DIM), F32).at[0, :h].set(bias)
    return pl.pallas_call(
        _gate_kernel,
        grid=(t // tm,),
        in_specs=[pl.BlockSpec((tm, d), lambda i: (i, 0)),
                  pl.BlockSpec((d, HEAD_DIM), lambda i: (0, 0)),
                  pl.BlockSpec((1, HEAD_DIM), lambda i: (0, 0))],
        out_specs=[pl.BlockSpec((tm, HEAD_DIM), lambda i: (i, 0)),
                   pl.BlockSpec((16, tm), lambda i: (0, i))],
        out_shape=[jax.ShapeDtypeStruct((t, HEAD_DIM), F32),
                   jax.ShapeDtypeStruct((16, t), F32)],
        scratch_shapes=[pltpu.VMEM((1, HEAD_DIM), F32)],
        compiler_params=_params(("arbitrary",), 32),
        name="forget_cumsum",
    )(x, w_pad, b_pad)


def _stick_kernel(q_ref, k_ref, v_ref, o_ref):
    tq = q_ref.shape[0]
    i = pl.program_id(1)
    q = q_ref[...]
    row = lax.broadcasted_iota(I32, (tq, tq), 0)
    col = lax.broadcasted_iota(I32, (tq, tq), 1)
    strict = col < row
    later = (row > col).astype(BF16)

    def block(j, c, acc, masked):
        start = pl.multiple_of(j * tq, tq)
        kb = k_ref[pl.ds(start, tq), :]
        vb = v_ref[pl.ds(start, tq), :]
        z = _dot_nt(q, kb)
        soft =jnp.log(1.0 + jnp.exp(-jnp.abs(z)))
        log_beta = jnp.minimum(z, 0.0) - soft
        log_stay = jnp.minimum(-z, 0.0) - soft
        if masked:
            log_stay = jnp.where(strict, log_stay, 0.0)
        hi, lo = _split2(log_stay)
        log_after = _dot(hi, later) + _dot(lo, later) + c
        w = jnp.exp(log_beta + log_after)
        if masked:
            w = jnp.where(strict, w, 0.0)
        acc = acc + _dot(w.astype(BF16), vb)
        c = c + jnp.sum(log_stay, axis=1, keepdims=True)
        return c, acc

    c0 = jnp.zeros((tq, 1), F32)
    acc0 = jnp.zeros((tq, HEAD_DIM), F32)
    c1, acc1 = block(i, c0, acc0, True)

    def cond(state):
        j, alive, _, _ = state
        return jnp.logical_and(j >= 0, alive > 0)

    def body(state):
        j, _, c, acc = state
        c, acc = block(j, c, acc, False)
        alive = (jnp.max(c) > EXP_ZERO).astype(I32)
        return j - 1, alive, c, acc

    alive1 = (jnp.max(c1) > EXP_ZERO).astype(I32)
    _, _, _, acc = lax.while_loop(cond, body, (i - 1, alive1, c1, acc1))
    o_ref[...] = acc.astype(o_ref.dtype)


def _stick_attention(proj, n_heads):
    t = proj.shape[0]
    tq = min(ATT_TILE, t)
    return pl.pallas_call(
        _stick_kernel,
        grid=(n_heads, t // tq),
        in_specs=[pl.BlockSpec((tq, HEAD_DIM), lambda h, i: (i, h)),
                  pl.BlockSpec((t, HEAD_DIM), lambda h, i: (0, n_heads + h)),
                  pl.BlockSpec((t, HEAD_DIM), lambda h, i: (0, 2 * n_heads + h))],
        out_specs=pl.BlockSpec((tq, HEAD_DIM), lambda h, i: (i, h)),
        out_shape=jax.ShapeDtypeStruct((t, n_heads * HEAD_DIM), BF16),
        compiler_params=_params(("arbitrary", "arbitrary"), 40),
        name="stick_attention",
    )(proj, proj, proj)


def _fox_kernel(q_ref, k_ref, v_ref, fq_ref, fk_ref, o_ref, kmax_ref):
    tq = q_ref.shape[0]
    h = pl.program_id(0)
    i = pl.program_id(1)

    @pl.when(i == 0)
    def _():
        kk = k_ref[...].astype(F32)
        norm2 = jnp.max(jnp.sum(kk * kk, axis=1, keepdims=True), axis=0, keepdims=True)
        kmax_ref[...] = jnp.broadcast_to(jnp.sqrt(norm2), kmax_ref.shape)

    q = q_ref[...]
    lane = lax.broadcasted_iota(I32, fq_ref.shape, 1)
    fq = jnp.sum(jnp.where(lane == h, fq_ref[...], 0.0), axis=1, keepdims=True)
    qf = q.astype(F32)
    reach = jnp.sqrt(jnp.sum(qf * qf, axis=1, keepdims=True)) * kmax_ref[0:1, 0:1] + fq
    row = lax.broadcasted_iota(I32, (tq, tq), 0)
    col = lax.broadcasted_iota(I32, (tq, tq), 1)
    causal = col <= row

    def block(j, m, l, acc, masked):
        start = pl.multiple_of(j * tq, tq)
        kb = k_ref[pl.ds(start, tq), :]
        vb = v_ref[pl.ds(start, tq), :]
        fk = fk_ref[0, j]
        s = _dot_nt(q, kb) + (fq - fk)
        if masked:
            s = jnp.where(causal, s, NEG_INF)
        m_new = jnp.maximum(m, jnp.max(s, axis=1, keepdims=True))
        alpha = jnp.exp(m - m_new)
        p = jnp.exp(s - m_new)
        l = alpha * l + jnp.sum(p, axis=1, keepdims=True)
        acc = alpha * acc + _dot(p.astype(BF16), vb)
        return m_new, l, acc

    m0 = jnp.full((tq, 1), NEG_INF, F32)
    l0 = jnp.zeros((tq, 1), F32)
    acc0 = jnp.zeros((tq, HEAD_DIM), F32)
    m1, l1, acc1 = block(i, m0, l0, acc0, True)

    def alive_before(jb, m):
        last = fk_ref[0, jnp.maximum(jb - 1, 0)][:, tq - 1:tq]
        return (jnp.max(reach - last - m) > EXP_ZERO - 2.0).astype(I32)

    def cond(state):
        jb, alive, _, _, _ = state
        return jnp.logical_and(jb >= 0, alive > 0)

    def body(state):
        jb, _, m, l, acc = state
        m, l, acc = block(jb, m, l, acc, False)
        return jb - 1, alive_before(jb, m), m, l, acc

    _, _, _, l, acc = lax.while_loop(cond, body, (i - 1, alive_before(i, m1), m1, l1, acc1))
    o_ref[...] = (acc * (1.0 / l)).astype(o_ref.dtype)


def _fox_attention(proj, cum, cum_t, n_heads):
    t = proj.shape[0]
    tq = min(FOX_TILE, t)
    fk = cum_t.reshape(cum_t.shape[0], t // tq, 1, tq)
    return pl.pallas_call(
        _fox_kernel,
        grid=(n_heads, t // tq),
        in_specs=[pl.BlockSpec((tq, HEAD_DIM), lambda h, i: (i, h)),
                  pl.BlockSpec((t, HEAD_DIM), lambda h, i: (0, n_heads + h)),
                  pl.BlockSpec((t, HEAD_DIM), lambda h, i: (0, 2 * n_heads + h)),
                  pl.BlockSpec((tq, HEAD_DIM), lambda h, i: (i, 0)),
                  pl.BlockSpec((1, t // tq, 1, tq), lambda h, i: (h, 0, 0, 0))],
        out_specs=pl.BlockSpec((tq, HEAD_DIM), lambda h, i: (i, h)),
        out_shape=jax.ShapeDtypeStruct((t, n_heads * HEAD_DIM), BF16),
        scratch_shapes=[pltpu.VMEM((8, HEAD_DIM), F32)],
        compiler_params=_params(("arbitrary", "arbitrary"), 40),
        name="fox_attention",
    )(proj, proj, proj, cum, fk)


def _mem_attn_kernel(q_ref, k_ref, v_ref, o_ref):
    s = _dot_nt(q_ref[...], k_ref[...])
    m = jnp.max(s, axis=1, keepdims=True)
    p = jnp.exp(s - m)
    l = jnp.sum(p, axis=1, keepdims=True)
    o_ref[...] = (_dot(p.astype(BF16), v_ref[...]) * (1.0 / l)).astype(o_ref.dtype)


def _memory_attention(proj, kv, q_col0):
    t = proj.shape[0]
    n_mem = kv.shape[0]
    tq = min(1024, t)
    c0 = q_col0 // HEAD_DIM
    return pl.pallas_call(
        _mem_attn_kernel,
        grid=(MEM_HEADS, t // tq),
        in_specs=[pl.BlockSpec((tq, HEAD_DIM), lambda h, i: (i, c0 + h)),
                  pl.BlockSpec((n_mem, HEAD_DIM), lambda h, i: (0, h)),
                  pl.BlockSpec((n_mem, HEAD_DIM), lambda h, i: (0, MEM_HEADS + h))],
        out_specs=pl.BlockSpec((tq, HEAD_DIM), lambda h, i: (i, h)),
        out_shape=jax.ShapeDtypeStruct((t, MEM_HEADS * HEAD_DIM), BF16),
        compiler_params=_params(("arbitrary", "arbitrary"), 32),
        name="memory_attention",
    )(proj, kv, kv)


def _out_proj_kernel(om_ref, oc_ref, w_ref, x_ref, g_ref, b_ref, y_ref, yp_ref):
    n_mix = om_ref.shape[1]
    mix = _dot(om_ref[...], w_ref[:n_mix, :]) + _dot(oc_ref[...], w_ref[n_mix:, :])
    y = _layer_norm(ALPHA * x_ref[...] + mix, g_ref[...], b_ref[...])
    y_ref[...] = y
    _store_packed_rows(yp_ref, y)


def _out_proj_norm(o_mix, o_mem, w_o_bf, x, g, b):
    t, d = x.shape
    tm = min(512, t)
    n_mix, n_mem = o_mix.shape[1], o_mem.shape[1]
    assert d == 2 * ROW_TILE * HEAD_DIM
    return pl.pallas_call(
        _out_proj_kernel,
        grid=(t // tm,),
        in_specs=[pl.BlockSpec((tm, n_mix), lambda i: (i, 0)),
                  pl.BlockSpec((tm, n_mem), lambda i: (i, 0)),
                  pl.BlockSpec((n_mix + n_mem, d), lambda i: (0, 0)),
                  pl.BlockSpec((tm, d), lambda i: (i, 0)),
                  pl.BlockSpec((1, d), lambda i: (0, 0)),
                  pl.BlockSpec((1, d), lambda i: (0, 0))],
        out_specs=[pl.BlockSpec((tm, d), lambda i: (i, 0)),
                   pl.BlockSpec((tm * ROW_TILE, HEAD_DIM), lambda i: (i, 0))],
        out_shape=[jax.ShapeDtypeStruct((t, d), F32),
                   jax.ShapeDtypeStruct((t * ROW_TILE, HEAD_DIM), U32)],
        compiler_params=_params(("arbitrary",), 48),
        name="out_proj_norm",
    )(o_mix, o_mem, w_o_bf, x, g.reshape(1, d), b.reshape(1, d))


def _router_kernel(x_ref, wh_ref, wl_ref, b_ref, eid_ref, rank_ref, gate_ref, cnt_ref, carry_ref):
    tm = x_ref.shape[0]

    @pl.when(pl.program_id(0) == 0)
    def _():
        carry_ref[...] = jnp.zeros_like(carry_ref)

    xh, xl = _split2(x_ref[...])
    wh, wl = wh_ref[...], wl_ref[...]
    logits = _dot_nt(wh, xh) + _dot_nt(wh, xl) + _dot_nt(wl, xh)
    scores = 1.0 / (1.0 + jnp.exp(-logits))
    sel = scores + b_ref[:, 0:1]

    group_score = []
    for g in range(N_GROUPS):
        v = sel[g * GROUP_SIZE:(g + 1) * GROUP_SIZE, :]
        m1 = jnp.max(v, axis=0, keepdims=True)
        is_max = v == m1
        n_max = jnp.sum(is_max.astype(F32), axis=0, keepdims=True)
        m2 = jnp.max(jnp.where(is_max, NEG_INF, v), axis=0, keepdims=True)
        group_score.append(m1 + jnp.where(n_max >= 2.0, m1, m2))
    masked = []
    for g in range(N_GROUPS):
        ahead = jnp.zeros((1, tm), F32)
        for g2 in range(N_GROUPS):
            if g2 == g:
                continue
            beats = group_score[g2] > group_score[g]
            if g2 < g:
                beats = jnp.logical_or(beats, group_score[g2] == group_score[g])
            ahead = ahead + beats.astype(F32)
        v = sel[g * GROUP_SIZE:(g + 1) * GROUP_SIZE, :]
        masked.append(jnp.where(ahead < float(TOPK_GROUPS), v, NEG_INF))
    msel = jnp.concatenate(masked, axis=0)

    e_idx = lax.broadcasted_iota(I32, (N_EXPERTS, tm), 0)
    ahead = jnp.zeros((N_EXPERTS, tm), F32)
    for e2 in range(N_EXPERTS):
        other = msel[e2:e2 + 1, :]
        beats = jnp.logical_or(other > msel, jnp.logical_and(other == msel, e_idx > e2))
        ahead = ahead + beats.astype(F32)
    chosen = ahead < float(TOP_K)
    chosen_f = chosen.astype(F32)

    w = jnp.where(chosen, scores, 0.0)
    gates = w / jnp.sum(w, axis=0, keepdims=True) * ROUTED_SCALE

    chosen_b = chosen_f.astype(BF16)
    r64 = lax.broadcasted_iota(I32, (N_EXPERTS, N_EXPERTS), 0)
    c64 = lax.broadcasted_iota(I32, (N_EXPERTS, N_EXPERTS), 1)
    choice = _dot((c64 < r64).astype(BF16), chosen_b)
    rt = lax.broadcasted_iota(I32, (tm, tm), 0)
    ct = lax.broadcasted_iota(I32, (tm, tm), 1)
    rank = _dot(chosen_b, (rt < ct).astype(BF16)) + carry_ref[:, 0:1]
    carry_ref[...] = carry_ref[...] + jnp.sum(chosen_f, axis=1, keepdims=True)
    cnt_ref[...] = carry_ref[...].astype(I32)

    e_f = e_idx.astype(F32)
    eids, ranks, gts = [], [], []
    for k in range(TOP_K):
        pick = jnp.logical_and(chosen, choice == float(k))
        eids.append(jnp.sum(jnp.where(pick, e_f, 0.0), axis=0, keepdims=True))
        ranks.append(jnp.sum(jnp.where(pick, rank, 0.0), axis=0, keepdims=True))
        gts.append(jnp.sum(jnp.where(pick, gates, 0.0), axis=0, keepdims=True))
    eid_ref[...] = jnp.concatenate(eids, axis=0).astype(I32)
    rank_ref[...] = jnp.concatenate(ranks, axis=0).astype(I32)
    gate_ref[...] = jnp.concatenate(gts, axis=0)


def _router(x1, w_router, router_bias):
    t, d = x1.shape
    tm = min(512, t)
    wt = w_router.T
    wh, wl = _split2(wt)
    bias = jnp.broadcast_to(router_bias.astype(F32)[:, None], (N_EXPERTS, HEAD_DIM))
    return pl.pallas_call(
        _router_kernel,
        grid=(t // tm,),
        in_specs=[pl.BlockSpec((tm, d), lambda i: (i, 0)),
                  pl.BlockSpec((N_EXPERTS, d), lambda i: (0, 0)),
                  pl.BlockSpec((N_EXPERTS, d), lambda i: (0, 0)),
                  pl.BlockSpec((N_EXPERTS, HEAD_DIM), lambda i: (0, 0))],
        out_specs=[pl.BlockSpec((TOP_K, tm), lambda i: (0, i)),
                   pl.BlockSpec((TOP_K, tm), lambda i: (0, i)),
                   pl.BlockSpec((TOP_K, tm), lambda i: (0, i)),
                   pl.BlockSpec((N_EXPERTS, HEAD_DIM), lambda i: (0, 0))],
        out_shape=[jax.ShapeDtypeStruct((TOP_K, t), I32),
                   jax.ShapeDtypeStruct((TOP_K, t), I32),
                   jax.ShapeDtypeStruct((TOP_K, t), F32),
                   jax.ShapeDtypeStruct((N_EXPERTS, HEAD_DIM), I32)],
        scratch_shapes=[pltpu.VMEM((N_EXPERTS, HEAD_DIM), F32)],
        compiler_params=_params(("arbitrary",), 40),
        name="router",
    )(x1, wh, wl, bias)


def _expert_kernel(te_ref, nu_ref, meta_hbm, x_hbm, wgu_ref, wd_ref, y_hbm,
                   meta_s, xa, xb, ya, yb, wgu_bf, wd_bf, msem, gsem, ssem, *, n_tok):
    tm = xa.shape[0] // ROW_TILE
    ff = wd_ref.shape[0]
    j = pl.program_id(0)
    nu = nu_ref[0]

    def meta_copy(a):
        slot = a % META_RING
        return pltpu.make_async_copy(meta_hbm.at[pl.ds(a * 2 * tm, 2 * tm)],
                                     meta_s.at[pl.ds(slot * 2 * tm, 2 * tm)], msem.at[slot])

    def start_gathers(a, x_dst, sem):
        base = (a % META_RING) * 2 * tm
        for r in range(tm):
            src = pl.multiple_of(meta_s[base + r], ROW_TILE)
            pltpu.make_async_copy(x_hbm.at[pl.ds(src, ROW_TILE), :],
                                  x_dst.at[pl.ds(r * ROW_TILE, ROW_TILE), :], sem).start()

    def start_scatters(a, y_src, sem):
        base = (a % META_RING) * 2 * tm + tm
        for r in range(tm):
            dest = pl.multiple_of(meta_s[base + r], ROW_TILE)
            pltpu.make_async_copy(y_src.at[pl.ds(r * ROW_TILE, ROW_TILE), :],
                                  y_hbm.at[pl.ds(dest, ROW_TILE), :], sem).start()

    def wait_gathers(x_dst, sem):
        pltpu.make_async_copy(x_hbm.at[pl.ds(0, tm * ROW_TILE), :], x_dst, sem).wait()

    def wait_scatters(y_src, sem):
        pltpu.make_async_copy(y_src, y_hbm.at[pl.ds(0, tm * ROW_TILE), :], sem).wait()

    @pl.when(j == 0)
    def _():
        ya[...] = jnp.zeros_like(ya)
        yb[...] = jnp.zeros_like(yb)
        pltpu.make_async_copy(ya, y_hbm.at[pl.ds(TOP_K * n_tok * ROW_TILE, tm * ROW_TILE), :],
                              ssem.at[0]).start()
        meta_copy(0).start()
        meta_copy(1).start()
        meta_copy(2).start()
        meta_copy(0).wait()
        meta_copy(1).wait()
        start_gathers(1, xa, gsem.at[0])

    def step(x_cur, y_cur, x_nxt, y_prv, p):
        wait_gathers(x_cur, gsem.at[p])
        wait_scatters(y_cur, ssem.at[p])
        start_gathers(j + 2, x_nxt, gsem.at[1 - p])
        start_scatters(j, y_prv, ssem.at[1 - p])
        gu = _dot(_load_packed_rows(x_cur).astype(BF16), wgu_bf[...])
        a = gu[:, :ff]
        u = gu[:, ff:]
        hidden = (a * (1.0 / (1.0 + jnp.exp(-a))) * u).astype(BF16)
        _store_packed_rows(y_cur, _dot(hidden, wd_bf[...]))

        @pl.when(j == nu - 1)
        def _():
            start_scatters(j + 1, y_cur, ssem.at[p])
            wait_scatters(y_prv, ssem.at[1 - p])
            wait_scatters(y_cur, ssem.at[p])
            wait_gathers(x_nxt, gsem.at[1 - p])
            meta_copy(j + 3).wait()

    @pl.when(j < nu)
    def _():
        meta_copy(j + 3).start()
        meta_copy(j + 2).wait()
        new_expert = jnp.logical_or(j == 0, te_ref[j] != te_ref[jnp.maximum(j - 1, 0)])

        @pl.when(new_expert)
        def _():
            wgu_bf[...] = wgu_ref[...].astype(BF16)
            wd_bf[...] = wd_ref[...].astype(BF16)

        @pl.when(j % 2 == 0)
        def _():
            step(xa, ya, xb, yb, 0)

        @pl.when(j % 2 == 1)
        def _():
            step(xb, yb, xa, ya, 1)


def _routed_experts(x_rows, meta, tile_expert, n_used, w_gu, w_down):
    t = x_rows.shape[0] // ROW_TILE
    d = w_gu.shape[1]
    ff = w_down.shape[1]
    tm = EXPERT_TILE
    n_tiles = tile_expert.shape[0]
    kern = functools.partial(_expert_kernel, n_tok=t)
    row_buf = pltpu.VMEM((tm * ROW_TILE, HEAD_DIM), U32)
    grid_spec = pltpu.PrefetchScalarGridSpec(
        num_scalar_prefetch=2,
        grid=(n_tiles,),
        in_specs=[pl.BlockSpec(memory_space=pl.ANY),
                  pl.BlockSpec(memory_space=pl.ANY),
                  pl.BlockSpec((None, d, 2 * ff), lambda j, te, nu: (te[j], 0, 0)),
                  pl.BlockSpec((None, ff, d), lambda j, te, nu: (te[j], 0, 0))],
        out_specs=pl.BlockSpec(memory_space=pl.ANY),
        scratch_shapes=[pltpu.SMEM((META_RING * 2 * tm,), I32),
                        row_buf, row_buf, row_buf, row_buf,
                        pltpu.VMEM((d, 2 * ff), BF16),
                        pltpu.VMEM((ff, d), BF16),
                        pltpu.SemaphoreType.DMA((META_RING,)),
                        pltpu.SemaphoreType.DMA((2,)),
                        pltpu.SemaphoreType.DMA((2,))],
    )
    return pl.pallas_call(
        kern,
        grid_spec=grid_spec,
        out_shape=jax.ShapeDtypeStruct(((TOP_K * t + 2 * tm) * ROW_TILE, HEAD_DIM), U32),
        compiler_params=_params(("arbitrary",), 56),
        name="routed_experts",
    )(tile_expert, n_used, meta, x_rows, w_gu, w_down)


def _dispatch_plan(eid, rank, counts, t):
    tm = EXPERT_TILE
    n_tiles = (TOP_K * t) // tm + N_EXPERTS
    n_slots = n_tiles * tm
    tok_bits = max((t - 1).bit_length(), 1)
    padded = ((counts + tm - 1) // tm) * tm
    ends = jnp.cumsum(padded)
    offs = ends - padded
    n_used = (ends[-1] // tm).astype(I32)
    tiles = jnp.arange(n_tiles, dtype=I32)
    experts = jnp.arange(N_EXPERTS, dtype=I32)
    tile_start = jnp.minimum(tiles, n_used - 1) * tm
    tile_expert = jnp.sum((ends[None, :] <= tile_start[:, None]).astype(I32), axis=1)
    tile_expert = jnp.clip(tile_expert, 0, N_EXPERTS - 1)
    of_tile = tile_expert[:, None] == experts[None, :]
    tile_count = jnp.sum(jnp.where(of_tile, counts[None, :], 0), axis=1)
    tile_offs = jnp.sum(jnp.where(of_tile, offs[None, :], 0), axis=1)
    tile_valid = jnp.clip(tile_count - (tiles * tm - tile_offs), 0, tm).astype(I32)
    slot = rank + jnp.sum(jnp.where(eid[:, :, None] == experts, offs, 0), axis=2)
    tok = lax.broadcasted_iota(I32, (TOP_K, t), 1)
    k = lax.broadcasted_iota(I32, (TOP_K, t), 0)
    packed = jnp.zeros((n_slots,), I32)
    packed = packed.at[slot.reshape(-1)].set((tok | (k << tok_bits)).reshape(-1), unique_indices=True)
    packed = packed.reshape(n_tiles, tm)
    row = lax.broadcasted_iota(I32, (n_tiles, tm), 1)
    gather_rows = packed & ((1 << tok_bits) - 1)
    real = (packed >> tok_bits) * t + gather_rows
    spare = TOP_K * t + (tiles[:, None] % 2) * tm + row
    scatter_rows = jnp.where(row < tile_valid[:, None], real, spare)
    plan = jnp.stack([gather_rows, scatter_rows], axis=1)
    dummy = jnp.stack([jnp.zeros((1, tm), I32), TOP_K * t + tm + row[:1]], axis=1)
    tail = jnp.concatenate([dummy, dummy], axis=0)
    meta = jnp.concatenate([dummy, plan, tail], axis=0).reshape(-1) * ROW_TILE
    return meta, tile_expert, n_used.reshape(1)


def _combine_kernel(x_ref, gate_ref, wgu_ref, wd_ref, g_ref, b_ref, *refs):
    y_refs = refs[:TOP_K]
    o_ref, ob_ref = refs[TOP_K:]
    ff = wd_ref.shape[0]
    x = x_ref[...]
    gu = _dot(x.astype(BF16), wgu_ref[...])
    a = gu[:, :ff]
    u = gu[:, ff:]
    hidden = (a * (1.0 / (1.0 + jnp.exp(-a))) * u).astype(BF16)
    ffn = _dot(hidden, wd_ref[...])
    gates = gate_ref[...]
    for k in range(TOP_K):
        ffn = ffn + gates[:, k:k + 1] * _load_packed_rows(y_refs[k])
    out = _layer_norm(ALPHA * x + ffn, g_ref[...], b_ref[...])
    o_ref[...] = out
    ob_ref[...] = out.astype(BF16)


def _combine_norm(x1, gates_tk, y, w_gu_bf, w_down_bf, g, b):
    t, d = x1.shape
    ff = w_down_bf.shape[0]
    tm = min(256, t)
    nb = t // tm
    y_specs = [pl.BlockSpec((tm * ROW_TILE, HEAD_DIM), functools.partial(lambda i, k: (k * nb + i, 0), k=k))
               for k in range(TOP_K)]
    return pl.pallas_call(
        _combine_kernel,
        grid=(nb,),
        in_specs=[pl.BlockSpec((tm, d), lambda i: (i, 0)),
                  pl.BlockSpec((tm, TOP_K), lambda i: (i, 0)),
                  pl.BlockSpec((d, 2 * ff), lambda i: (0, 0)),
                  pl.BlockSpec((ff, d), lambda i: (0, 0)),
                  pl.BlockSpec((1, d), lambda i: (0, 0)),
                  pl.BlockSpec((1, d), lambda i: (0, 0))] + y_specs,
        out_specs=[pl.BlockSpec((tm, d), lambda i: (i, 0)),
                   pl.BlockSpec((tm, d), lambda i: (i, 0))],
        out_shape=[jax.ShapeDtypeStruct((t, d), F32),
                   jax.ShapeDtypeStruct((t, d), BF16)],
        compiler_params=_params(("arbitrary",), 48),
        name="combine_norm",
    )(x1, gates_tk, w_gu_bf, w_down_bf, g.reshape(1, d), b.reshape(1, d), *([y] * TOP_K))


def _moe_block(x1, x1_rows, router, router_bias, exp_w_gu, exp_w_down, shared_w_gu, shared_w_down,
               g, b):
    t = x1.shape[0]
    eid, rank, gate, counts = _router(x1, router, router_bias)
    meta, tile_expert, n_used = _dispatch_plan(eid, rank, counts[:, 0], t)
    y = _routed_experts(x1_rows, meta, tile_expert, n_used, exp_w_gu, exp_w_down)
    return _combine_norm(x1, gate.T, y, shared_w_gu.astype(BF16), shared_w_down.astype(BF16), g, b)


def _layer(x, xb, kv, mixer, w_in, forget_bias, w_o, ln_attn_g, ln_attn_b, router, router_bias,
           exp_w_gu, exp_w_down, shared_w_gu, shared_w_down, ln_ffn_g, ln_ffn_b):
    mix_w = MIX_HEADS * HEAD_DIM
    n_proj = 3 * mix_w + MEM_HEADS * HEAD_DIM
    proj = _in_proj(xb, w_in, n_proj, mix_w, 3 * mix_w)
    if mixer == 0:
        o_mix = _stick_attention(proj, MIX_HEADS)
    else:
        cum, cum_t = _forget_cumsum(x, w_in[:, n_proj:], forget_bias)
        o_mix = _fox_attention(proj, cum, cum_t, MIX_HEADS)
    o_mem = _memory_attention(proj, kv, 3 * mix_w)
    x1, x1_rows = _out_proj_norm(o_mix, o_mem, w_o.astype(BF16), x, ln_attn_g, ln_attn_b)
    return _moe_block(x1, x1_rows, router, router_bias, exp_w_gu, exp_w_down, shared_w_gu,
                      shared_w_down, ln_ffn_g, ln_ffn_b)


def kernel(x, mem, mem_ln_g, mem_ln_b, w_mem_kv,
           l0_w_in, l0_w_o, l0_ln_attn_g, l0_ln_attn_b, l0_router, l0_router_bias,
           l0_exp_w_gu, l0_exp_w_down, l0_shared_w_gu, l0_shared_w_down, l0_ln_ffn_g, l0_ln_ffn_b,
           l1_w_in, l1_forget_bias, l1_w_o, l1_ln_attn_g, l1_ln_attn_b, l1_router, l1_router_bias,
           l1_exp_w_gu, l1_exp_w_down, l1_shared_w_gu, l1_shared_w_down, l1_ln_ffn_g, l1_ln_ffn_b):
    batch, seq, d = x.shape
    assert batch == 1
    kv = _memory_kv(mem[0], mem_ln_g, mem_ln_b, w_mem_kv)
    x2 = x[0]
    x2, xb = _layer(x2, x2.astype(BF16), kv, 0, l0_w_in, None, l0_w_o, l0_ln_attn_g, l0_ln_attn_b,
                    l0_router, l0_router_bias, l0_exp_w_gu, l0_exp_w_down, l0_shared_w_gu,
                    l0_shared_w_down, l0_ln_ffn_g, l0_ln_ffn_b)
    x2, _ = _layer(x2, xb, kv, 1, l1_w_in, l1_forget_bias, l1_w_o, l1_ln_attn_g, l1_ln_attn_b,
                   l1_router, l1_router_bias, l1_exp_w_gu, l1_exp_w_down, l1_shared_w_gu,
                   l1_shared_w_down, l1_ln_ffn_g, l1_ln_ffn_b)
    return x2.reshape(batch, seq, d)
```

```python
import functools

import jax
import jax.numpy as jnp
from jax import lax
from jax.experimental import pallas as pl
from jax.experimental.pallas import tpu as pltpu

F32 = jnp.float32
BF16 = jnp.bfloat16
I32 = jnp.int32

HEAD_DIM = 128
MIX_HEADS = 12
MEM_HEADS = 4
N_EXPERTS = 64
TOP_K = 8
N_GROUPS = 8
GROUP_SIZE = N_EXPERTS // N_GROUPS
TOPK_GROUPS = 4
ROUTED_SCALE = 2.5
DEPTH = 2
ALPHA = float((2 * DEPTH) ** 0.25)
LN_EPS = 1e-5

MIB = 1024 * 1024
ATT_TILE = 256
FOX_TILE = 512
STICK_HEADS = 2
EXPERT_TILE = 256
META_RING = 4
ROW_TILE = 16
EXP_ZERO = -104.0
NEG_INF = float("-inf")


def _params(semantics, vmem_mib):
    return pltpu.CompilerParams(dimension_semantics=semantics, vmem_limit_bytes=vmem_mib * MIB)


def _dot(a, b):
    return jnp.dot(a, b, preferred_element_type=F32)


def _dot_nt(a, b):
    return lax.dot_general(a, b, (((1,), (1,)), ((), ())), preferred_element_type=F32)


def _split2(x):
    hi = x.astype(BF16)
    lo = (x - hi.astype(F32)).astype(BF16)
    return hi, lo


def _split3(x):
    hi = x.astype(BF16)
    r = x - hi.astype(F32)
    mid = r.astype(BF16)
    lo = (r - mid.astype(F32)).astype(BF16)
    return hi, mid, lo


def _store_rows(ref, y):
    rows = y.shape[0]
    for c in range(ROW_TILE):
        ref[pl.ds(c, rows, stride=ROW_TILE), :] = y[:, c * HEAD_DIM:(c + 1) * HEAD_DIM]


def _load_rows(ref):
    rows = ref.shape[0] // ROW_TILE
    return jnp.concatenate([ref[pl.ds(c, rows, stride=ROW_TILE), :] for c in range(ROW_TILE)],
                           axis=1)


def _log_sigmoid(z):
    return jnp.minimum(z, 0.0) - jnp.log(1.0 + jnp.exp(-jnp.abs(z)))


def _layer_norm(y, g, b):
    mu = jnp.mean(y, axis=-1, keepdims=True)
    d = y - mu
    var = jnp.mean(d * d, axis=-1, keepdims=True)
    return d * lax.rsqrt(var + LN_EPS) * g + b


def _kv_kernel(mem_ref, g_ref, b_ref, w_ref, o_ref):
    y = _layer_norm(mem_ref[...], g_ref[...], b_ref[...])
    o_ref[...] = _dot(y.astype(BF16), w_ref[...].astype(BF16)).astype(o_ref.dtype)


def _memory_kv(mem, g, b, w):
    n, d = mem.shape
    nw = w.shape[1]
    tn = 512
    return pl.pallas_call(
        _kv_kernel,
        grid=(nw // tn,),
        in_specs=[pl.BlockSpec((n, d), lambda j: (0, 0)),
                  pl.BlockSpec((1, d), lambda j: (0, 0)),
                  pl.BlockSpec((1, d), lambda j: (0, 0)),
                  pl.BlockSpec((d, tn), lambda j: (0, j))],
        out_specs=pl.BlockSpec((n, tn), lambda j: (0, j)),
        out_shape=jax.ShapeDtypeStruct((n, nw), BF16),
        compiler_params=_params(("arbitrary",), 32),
        name="memory_kv",
    )(mem, g.reshape(1, d), b.reshape(1, d), w)


def _in_proj_kernel(x_ref, w_ref, o_ref, wbf_ref, *, k_col0, qmem_col0, scale):
    j = pl.program_id(0)
    tn = o_ref.shape[1]

    @pl.when(pl.program_id(1) == 0)
    def _():
        wbf_ref[...] = w_ref[...].astype(BF16)

    col0 = j * tn
    is_query = jnp.logical_or(col0 < k_col0, col0 >= qmem_col0)
    factor = jnp.where(is_query, scale, 1.0).astype(F32)
    o_ref[...] = (_dot(x_ref[...], wbf_ref[...]) * factor).astype(o_ref.dtype)


def _in_proj(xb, w, n_out, k_col0, qmem_col0):
    m, d = xb.shape
    tm, tn = 1024, 512
    tm = min(tm, m)
    assert k_col0 % tn == 0 and qmem_col0 % tn == 0
    kern = functools.partial(_in_proj_kernel, k_col0=k_col0, qmem_col0=qmem_col0,
                             scale=HEAD_DIM ** -0.5)
    return pl.pallas_call(
        kern,
        grid=(n_out // tn, m // tm),
        in_specs=[pl.BlockSpec((tm, d), lambda j, i: (i, 0)),
                  pl.BlockSpec((d, tn), lambda j, i: (0, j))],
        out_specs=pl.BlockSpec((tm, tn), lambda j, i: (i, j)),
        out_shape=jax.ShapeDtypeStruct((m, n_out), BF16),
        scratch_shapes=[pltpu.VMEM((d, tn), BF16)],
        compiler_params=_params(("arbitrary", "arbitrary"), 40),
        name="in_proj",
    )(xb, w)


def _gate_kernel(x_ref, w_ref, b_ref, cum_ref, cum_t_ref, carry_ref):
    tm = x_ref.shape[0]

    @pl.when(pl.program_id(0) == 0)
    def _():
        carry_ref[...] = jnp.zeros_like(carry_ref)

    xh, xl = _split2(x_ref[...])
    wh, wl = _split2(w_ref[...])
    f = _dot(xh, wh) + _dot(xh, wl) + _dot(xl, wh) + b_ref[...]
    lf = _log_sigmoid(f)
    row = lax.broadcasted_iota(I32, (tm, tm), 0)
    col = lax.broadcasted_iota(I32, (tm, tm), 1)
    lower = (col <= row).astype(BF16)
    p0, p1, p2 = _split3(lf)
    cum = _dot(lower, p0) + _dot(lower, p1) + _dot(lower, p2) + carry_ref[...]
    carry_ref[...] = cum[tm - 1:tm, :]
    cum_ref[...] = cum
    cum_t_ref[...] = cum.T[:cum_t_ref.shape[0], :]


def _forget_cumsum(x, w_f, bias):
    t, d = x.shape
    h = w_f.shape[1]
    tm = min(256, t)
    w_pad = jnp.zeros((d, HEAD_DIM), F32).at[:, :h].set(w_f)
    b_pad = jnp.zeros((1, HEAD_DIM), F32).at[0, :h].set(bias)
    return pl.pallas_call(
        _gate_kernel,
        grid=(t // tm,),
        in_specs=[pl.BlockSpec((tm, d), lambda i: (i, 0)),
                  pl.BlockSpec((d, HEAD_DIM), lambda i: (0, 0)),
                  pl.BlockSpec((1, HEAD_DIM), lambda i: (0, 0))],
        out_specs=[pl.BlockSpec((tm, HEAD_DIM), lambda i: (i, 0)),
                   pl.BlockSpec((16, tm), lambda i: (0, i))],
        out_shape=[jax.ShapeDtypeStruct((t, HEAD_DIM), F32),
                   jax.ShapeDtypeStruct((16, t), F32)],
        scratch_shapes=[pltpu.VMEM((1, HEAD_DIM), F32)],
        compiler_params=_params(("arbitrary",), 32),
        name="forget_cumsum",
    )(x, w_pad, b_pad)


def _stick_kernel(q_ref, k_ref, v_ref, o_ref):
    tq = q_ref.shape[0]
    i = pl.program_id(1)
    row = lax.broadcasted_iota(I32, (tq, tq), 0)
    col = lax.broadcasted_iota(I32, (tq, tq), 1)
    strict = col < row
    later = (row > col).astype(BF16)

    def block(j, c, acc, masked, head):
        start = pl.multiple_of(j * tq, tq)
        lanes = pl.ds(head * HEAD_DIM, HEAD_DIM)
        kb = k_ref[pl.ds(start, tq), lanes]
        vb = v_ref[pl.ds(start, tq), lanes]
        z = _dot_nt(q_ref[:, lanes], kb)
        soft =jnp.log(1.0 + jnp.exp(-jnp.abs(z)))
        log_beta = jnp.minimum(z, 0.0) - soft
        log_stay = jnp.minimum(-z, 0.0) - soft
        if masked:
            log_stay = jnp.where(strict, log_stay, 0.0)
        hi, lo = _split2(log_stay)
        log_after = _dot(hi, later) + _dot(lo, later) + c
        w = jnp.exp(log_beta + log_after)
        if masked:
            w = jnp.where(strict, w, 0.0)
        acc = acc + _dot(w.astype(BF16), vb)
        c = c + jnp.sum(log_stay, axis=1, keepdims=True)
        return c, acc

    heads = range(STICK_HEADS)
    c0 = jnp.zeros((tq, 1), F32)
    acc0 = jnp.zeros((tq, HEAD_DIM), F32)
    first = [block(i, c0, acc0, True, head) for head in heads]

    def any_alive(cs):
        worst = cs[0]
        for c in cs[1:]:
            worst = jnp.maximum(worst, c)
        return (jnp.max(worst) > EXP_ZERO).astype(I32)

    def cond(state):
        j, alive, _ = state
        return jnp.logical_and(j >= 0, alive > 0)

    def body(state):
        j, _, per_head = state
        per_head = [block(j, c, acc, False, head) for head, (c, acc) in zip(heads, per_head)]
        return j - 1, any_alive([c for c, _ in per_head]), per_head

    _, _, last = lax.while_loop(cond, body, (i - 1, any_alive([c for c, _ in first]), first))
    for head, (_, acc) in zip(heads, last):
        o_ref[:, pl.ds(head * HEAD_DIM, HEAD_DIM)] = acc.astype(o_ref.dtype)


def _stick_attention(proj, n_heads):
    t = proj.shape[0]
    tq = min(ATT_TILE, t)
    assert n_heads % STICK_HEADS == 0
    groups = n_heads // STICK_HEADS
    width = STICK_HEADS * HEAD_DIM
    return pl.pallas_call(
        _stick_kernel,
        grid=(groups, t // tq),
        in_specs=[pl.BlockSpec((tq, width), lambda h, i: (i, h)),
                  pl.BlockSpec((t, width), lambda h, i: (0, groups + h)),
                  pl.BlockSpec((t, width), lambda h, i: (0, 2 * groups + h))],
        out_specs=pl.BlockSpec((tq, width), lambda h, i: (i, h)),
        out_shape=jax.ShapeDtypeStruct((t, n_heads * HEAD_DIM), BF16),
        compiler_params=_params(("arbitrary", "arbitrary"), 40),
        name="stick_attention",
    )(proj, proj, proj)


def _fox_kernel(q_ref, k_ref, v_ref, fq_ref, fk_ref, o_ref, kmax_ref):
    tq = q_ref.shape[0]
    h = pl.program_id(0)
    i = pl.program_id(1)

    @pl.when(i == 0)
    def _():
        kk = k_ref[...].astype(F32)
        norm2 = jnp.max(jnp.sum(kk * kk, axis=1, keepdims=True), axis=0, keepdims=True)
        kmax_ref[...] = jnp.broadcast_to(jnp.sqrt(norm2), kmax_ref.shape)

    q = q_ref[...]
    lane = lax.broadcasted_iota(I32, fq_ref.shape, 1)
    fq = jnp.sum(jnp.where(lane == h, fq_ref[...], 0.0), axis=1, keepdims=True)
    qf = q.astype(F32)
    reach = jnp.sqrt(jnp.sum(qf * qf, axis=1, keepdims=True)) * kmax_ref[0:1, 0:1] + fq
    row = lax.broadcasted_iota(I32, (tq, tq), 0)
    col = lax.broadcasted_iota(I32, (tq, tq), 1)
    causal = col <= row

    def block(j, m, l, acc, masked):
        start = pl.multiple_of(j * tq, tq)
        kb = k_ref[pl.ds(start, tq), :]
        vb = v_ref[pl.ds(start, tq), :]
        fk = fk_ref[0, j]
        s = _dot_nt(q, kb) + (fq - fk)
        if masked:
            s = jnp.where(causal, s, NEG_INF)
        m_new = jnp.maximum(m, jnp.max(s, axis=1, keepdims=True))
        alpha = jnp.exp(m - m_new)
        p = jnp.exp(s - m_new)
        l = alpha * l + jnp.sum(p, axis=1, keepdims=True)
        acc = alpha * acc + _dot(p.astype(BF16), vb)
        return m_new, l, acc

    m0 = jnp.full((tq, 1), NEG_INF, F32)
    l0 = jnp.zeros((tq, 1), F32)
    acc0 = jnp.zeros((tq, HEAD_DIM), F32)
    m1, l1, acc1 = block(i, m0, l0, acc0, True)

    def alive_before(jb, m):
        last = fk_ref[0, jnp.maximum(jb - 1, 0)][:, tq - 1:tq]
        return (jnp.max(reach - last - m) > EXP_ZERO - 2.0).astype(I32)

    def cond(state):
        jb, alive, _, _, _ = state
        return jnp.logical_and(jb >= 0, alive > 0)

    def body(state):
        jb, _, m, l, acc = state
        m, l, acc = block(jb, m, l, acc, False)
        return jb - 1, alive_before(jb, m), m, l, acc

    _, _, _, l, acc = lax.while_loop(cond, body, (i - 1, alive_before(i, m1), m1, l1, acc1))
    o_ref[...] = (acc * (1.0 / l)).astype(o_ref.dtype)


def _fox_attention(proj, cum, cum_t, n_heads):
    t = proj.shape[0]
    tq = min(FOX_TILE, t)
    fk = cum_t.reshape(cum_t.shape[0], t // tq, 1, tq)
    return pl.pallas_call(
        _fox_kernel,
        grid=(n_heads, t // tq),
        in_specs=[pl.BlockSpec((tq, HEAD_DIM), lambda h, i: (i, h)),
                  pl.BlockSpec((t, HEAD_DIM), lambda h, i: (0, n_heads + h)),
                  pl.BlockSpec((t, HEAD_DIM), lambda h, i: (0, 2 * n_heads + h)),
                  pl.BlockSpec((tq, HEAD_DIM), lambda h, i: (i, 0)),
                  pl.BlockSpec((1, t // tq, 1, tq), lambda h, i: (h, 0, 0, 0))],
        out_specs=pl.BlockSpec((tq, HEAD_DIM), lambda h, i: (i, h)),
        out_shape=jax.ShapeDtypeStruct((t, n_heads * HEAD_DIM), BF16),
        scratch_shapes=[pltpu.VMEM((8, HEAD_DIM), F32)],
        compiler_params=_params(("arbitrary", "arbitrary"), 40),
        name="fox_attention",
    )(proj, proj, proj, cum, fk)


def _mem_attn_kernel(q_ref, k_ref, v_ref, o_ref):
    s = _dot_nt(q_ref[...], k_ref[...])
    m = jnp.max(s, axis=1, keepdims=True)
    p = jnp.exp(s - m)
    l = jnp.sum(p, axis=1, keepdims=True)
    o_ref[...] = (_dot(p.astype(BF16), v_ref[...]) * (1.0 / l)).astype(o_ref.dtype)


def _memory_attention(proj, kv, q_col0):
    t = proj.shape[0]
    n_mem = kv.shape[0]
    tq = min(1024, t)
    c0 = q_col0 // HEAD_DIM
    return pl.pallas_call(
        _mem_attn_kernel,
        grid=(MEM_HEADS, t // tq),
        in_specs=[pl.BlockSpec((tq, HEAD_DIM), lambda h, i: (i, c0 + h)),
                  pl.BlockSpec((n_mem, HEAD_DIM), lambda h, i: (0, h)),
                  pl.BlockSpec((n_mem, HEAD_DIM), lambda h, i: (0, MEM_HEADS + h))],
        out_specs=pl.BlockSpec((tq, HEAD_DIM), lambda h, i: (i, h)),
        out_shape=jax.ShapeDtypeStruct((t, MEM_HEADS * HEAD_DIM), BF16),
        compiler_params=_params(("arbitrary", "arbitrary"), 32),
        name="memory_attention",
    )(proj, kv, kv)


def _out_proj_kernel(om_ref, oc_ref, w_ref, x_ref, g_ref, b_ref, y_ref, yp_ref):
    n_mix = om_ref.shape[1]
    mix = _dot(om_ref[...], w_ref[:n_mix, :]) + _dot(oc_ref[...], w_ref[n_mix:, :])
    y = _layer_norm(ALPHA * x_ref[...] + mix, g_ref[...], b_ref[...])
    y_ref[...] = y
    _store_rows(yp_ref, y)


def _out_proj_norm(o_mix, o_mem, w_o_bf, x, g, b):
    t, d = x.shape
    tm = min(512, t)
    n_mix, n_mem = o_mix.shape[1], o_mem.shape[1]
    assert d == ROW_TILE * HEAD_DIM
    return pl.pallas_call(
        _out_proj_kernel,
        grid=(t // tm,),
        in_specs=[pl.BlockSpec((tm, n_mix), lambda i: (i, 0)),
                  pl.BlockSpec((tm, n_mem), lambda i: (i, 0)),
                  pl.BlockSpec((n_mix + n_mem, d), lambda i: (0, 0)),
                  pl.BlockSpec((tm, d), lambda i: (i, 0)),
                  pl.BlockSpec((1, d), lambda i: (0, 0)),
                  pl.BlockSpec((1, d), lambda i: (0, 0))],
        out_specs=[pl.BlockSpec((tm, d), lambda i: (i, 0)),
                   pl.BlockSpec((tm * ROW_TILE, HEAD_DIM), lambda i: (i, 0))],
        out_shape=[jax.ShapeDtypeStruct((t, d), F32),
                   jax.ShapeDtypeStruct((t * ROW_TILE, HEAD_DIM), F32)],
        compiler_params=_params(("arbitrary",), 48),
        name="out_proj_norm",
    )(o_mix, o_mem, w_o_bf, x, g.reshape(1, d), b.reshape(1, d))


def _router_kernel(x_ref, wh_ref, wl_ref, b_ref, eid_ref, rank_ref, gate_ref, cnt_ref, carry_ref):
    tm = x_ref.shape[0]

    @pl.when(pl.program_id(0) == 0)
    def _():
        carry_ref[...] = jnp.zeros_like(carry_ref)

    xh, xl = _split2(x_ref[...])
    wh, wl = wh_ref[...], wl_ref[...]
    logits = _dot_nt(wh, xh) + _dot_nt(wh, xl) + _dot_nt(wl, xh)
    scores = 1.0 / (1.0 + jnp.exp(-logits))
    sel = scores + b_ref[:, 0:1]

    group_score = []
    for g in range(N_GROUPS):
        v = sel[g * GROUP_SIZE:(g + 1) * GROUP_SIZE, :]
        m1 = jnp.max(v, axis=0, keepdims=True)
        is_max = v == m1
        n_max = jnp.sum(is_max.astype(F32), axis=0, keepdims=True)
        m2 = jnp.max(jnp.where(is_max, NEG_INF, v), axis=0, keepdims=True)
        group_score.append(m1 + jnp.where(n_max >= 2.0, m1, m2))
    masked = []
    for g in range(N_GROUPS):
        ahead = jnp.zeros((1, tm), F32)
        for g2 in range(N_GROUPS):
            if g2 == g:
                continue
            beats = group_score[g2] > group_score[g]
            if g2 < g:
                beats = jnp.logical_or(beats, group_score[g2] == group_score[g])
            ahead = ahead + beats.astype(F32)
        v = sel[g * GROUP_SIZE:(g + 1) * GROUP_SIZE, :]
        masked.append(jnp.where(ahead < float(TOPK_GROUPS), v, NEG_INF))
    msel = jnp.concatenate(masked, axis=0)

    e_idx = lax.broadcasted_iota(I32, (N_EXPERTS, tm), 0)
    ahead = jnp.zeros((N_EXPERTS, tm), F32)
    for e2 in range(N_EXPERTS):
        other = msel[e2:e2 + 1, :]
        beats = jnp.logical_or(other > msel, jnp.logical_and(other == msel, e_idx > e2))
        ahead = ahead + beats.astype(F32)
    chosen = ahead < float(TOP_K)
    chosen_f = chosen.astype(F32)

    w = jnp.where(chosen, scores, 0.0)
    gates = w / jnp.sum(w, axis=0, keepdims=True) * ROUTED_SCALE

    chosen_b = chosen_f.astype(BF16)
    r64 = lax.broadcasted_iota(I32, (N_EXPERTS, N_EXPERTS), 0)
    c64 = lax.broadcasted_iota(I32, (N_EXPERTS, N_EXPERTS), 1)
    choice = _dot((c64 < r64).astype(BF16), chosen_b)
    rt = lax.broadcasted_iota(I32, (tm, tm), 0)
    ct = lax.broadcasted_iota(I32, (tm, tm), 1)
    rank = _dot(chosen_b, (rt < ct).astype(BF16)) + carry_ref[:, 0:1]
    carry_ref[...] = carry_ref[...] + jnp.sum(chosen_f, axis=1, keepdims=True)
    cnt_ref[...] = carry_ref[...].astype(I32)

    e_f = e_idx.astype(F32)
    eids, ranks, gts = [], [], []
    for k in range(TOP_K):
        pick = jnp.logical_and(chosen, choice == float(k))
        eids.append(jnp.sum(jnp.where(pick, e_f, 0.0), axis=0, keepdims=True))
        ranks.append(jnp.sum(jnp.where(pick, rank, 0.0), axis=0, keepdims=True))
        gts.append(jnp.sum(jnp.where(pick, gates, 0.0), axis=0, keepdims=True))
    eid_ref[...] = jnp.concatenate(eids, axis=0).astype(I32)
    rank_ref[...] = jnp.concatenate(ranks, axis=0).astype(I32)
    gate_ref[...] = jnp.concatenate(gts, axis=0)


def _router(x1, w_router, router_bias):
    t, d = x1.shape
    tm = min(512, t)
    wt = w_router.T
    wh, wl = _split2(wt)
    bias = jnp.broadcast_to(router_bias.astype(F32)[:, None], (N_EXPERTS, HEAD_DIM))
    return pl.pallas_call(
        _router_kernel,
        grid=(t // tm,),
        in_specs=[pl.BlockSpec((tm, d), lambda i: (i, 0)),
                  pl.BlockSpec((N_EXPERTS, d), lambda i: (0, 0)),
                  pl.BlockSpec((N_EXPERTS, d), lambda i: (0, 0)),
                  pl.BlockSpec((N_EXPERTS, HEAD_DIM), lambda i: (0, 0))],
        out_specs=[pl.BlockSpec((TOP_K, tm), lambda i: (0, i)),
                   pl.BlockSpec((TOP_K, tm), lambda i: (0, i)),
                   pl.BlockSpec((TOP_K, tm), lambda i: (0, i)),
                   pl.BlockSpec((N_EXPERTS, HEAD_DIM), lambda i: (0, 0))],
        out_shape=[jax.ShapeDtypeStruct((TOP_K, t), I32),
                   jax.ShapeDtypeStruct((TOP_K, t), I32),
                   jax.ShapeDtypeStruct((TOP_K, t), F32),
                   jax.ShapeDtypeStruct((N_EXPERTS, HEAD_DIM), I32)],
        scratch_shapes=[pltpu.VMEM((N_EXPERTS, HEAD_DIM), F32)],
        compiler_params=_params(("arbitrary",), 40),
        name="router",
    )(x1, wh, wl, bias)


def _expert_kernel(te_ref, nu_ref, meta_hbm, x_hbm, wgu_ref, wd_ref, y_hbm,
                   meta_s, xa, xb, ya, yb, wgu_bf, wd_bf, msem, gsem, ssem, *, n_tok):
    tm = xa.shape[0] // ROW_TILE
    ff = wd_ref.shape[0]
    j = pl.program_id(0)
    nu = nu_ref[0]

    def meta_copy(a):
        slot = a % META_RING
        return pltpu.make_async_copy(meta_hbm.at[pl.ds(a * 2 * tm, 2 * tm)],
                                     meta_s.at[pl.ds(slot * 2 * tm, 2 * tm)], msem.at[slot])

    def start_gathers(a, x_dst, sem):
        base = (a % META_RING) * 2 * tm
        for r in range(tm):
            src = pl.multiple_of(meta_s[base + r], ROW_TILE)
            pltpu.async_copy(x_hbm.at[pl.ds(src, ROW_TILE), :],
                             x_dst.at[pl.ds(r * ROW_TILE, ROW_TILE), :], sem, priority=r % 2)

    def start_scatters(a, y_src, sem):
        base = (a % META_RING) * 2 * tm + tm
        for r in range(tm):
            dest = pl.multiple_of(meta_s[base + r], ROW_TILE)
            pltpu.async_copy(y_src.at[pl.ds(r * ROW_TILE, ROW_TILE), :],
                             y_hbm.at[pl.ds(dest, ROW_TILE), :], sem, priority=r % 2)

    def wait_gathers(x_dst, sem):
        pltpu.make_async_copy(x_hbm.at[pl.ds(0, tm * ROW_TILE), :], x_dst, sem).wait()

    def wait_scatters(y_src, sem):
        pltpu.make_async_copy(y_src, y_hbm.at[pl.ds(0, tm * ROW_TILE), :], sem).wait()

    @pl.when(j == 0)
    def _():
        ya[...] = jnp.zeros_like(ya)
        yb[...] = jnp.zeros_like(yb)
        pltpu.make_async_copy(ya, y_hbm.at[pl.ds(TOP_K * n_tok * ROW_TILE, tm * ROW_TILE), :],
                              ssem.at[0]).start()
        meta_copy(0).start()
        meta_copy(1).start()
        meta_copy(2).start()
        meta_copy(0).wait()
        meta_copy(1).wait()
        start_gathers(1, xa, gsem.at[0])

    def step(x_cur, y_cur, x_nxt, y_prv, p):
        wait_gathers(x_cur, gsem.at[p])
        wait_scatters(y_cur, ssem.at[p])
        start_gathers(j + 2, x_nxt, gsem.at[1 - p])
        start_scatters(j, y_prv, ssem.at[1 - p])
        gu = _dot(_load_rows(x_cur).astype(BF16), wgu_bf[...])
        a = gu[:, :ff]
        u = gu[:, ff:]
        hidden = (a * (1.0 / (1.0 + jnp.exp(-a))) * u).astype(BF16)
        _store_rows(y_cur, _dot(hidden, wd_bf[...]))

        @pl.when(j == nu - 1)
        def _():
            start_scatters(j + 1, y_cur, ssem.at[p])
            wait_scatters(y_prv, ssem.at[1 - p])
            wait_scatters(y_cur, ssem.at[p])
            wait_gathers(x_nxt, gsem.at[1 - p])
            meta_copy(j + 3).wait()

    @pl.when(j < nu)
    def _():
        meta_copy(j + 3).start()
        meta_copy(j + 2).wait()
        new_expert = jnp.logical_or(j == 0, te_ref[j] != te_ref[jnp.maximum(j - 1, 0)])

        @pl.when(new_expert)
        def _():
            wgu_bf[...] = wgu_ref[...].astype(BF16)
            wd_bf[...] = wd_ref[...].astype(BF16)

        @pl.when(j % 2 == 0)
        def _():
            step(xa, ya, xb, yb, 0)

        @pl.when(j % 2 == 1)
        def _():
            step(xb, yb, xa, ya, 1)


def _routed_experts(x_rows, meta, tile_expert, n_used, w_gu, w_down):
    t = x_rows.shape[0] // ROW_TILE
    d = w_gu.shape[1]
    ff = w_down.shape[1]
    tm = EXPERT_TILE
    n_tiles = tile_expert.shape[0]
    kern = functools.partial(_expert_kernel, n_tok=t)
    row_buf = pltpu.VMEM((tm * ROW_TILE, HEAD_DIM), F32)
    grid_spec = pltpu.PrefetchScalarGridSpec(
        num_scalar_prefetch=2,
        grid=(n_tiles,),
        in_specs=[pl.BlockSpec(memory_space=pl.ANY),
                  pl.BlockSpec(memory_space=pl.ANY),
                  pl.BlockSpec((None, d, 2 * ff), lambda j, te, nu: (te[j], 0, 0)),
                  pl.BlockSpec((None, ff, d), lambda j, te, nu: (te[j], 0, 0))],
        out_specs=pl.BlockSpec(memory_space=pl.ANY),
        scratch_shapes=[pltpu.SMEM((META_RING * 2 * tm,), I32),
                        row_buf, row_buf, row_buf, row_buf,
                        pltpu.VMEM((d, 2 * ff), BF16),
                        pltpu.VMEM((ff, d), BF16),
                        pltpu.SemaphoreType.DMA((META_RING,)),
                        pltpu.SemaphoreType.DMA((2,)),
                        pltpu.SemaphoreType.DMA((2,))],
    )
    return pl.pallas_call(
        kern,
        grid_spec=grid_spec,
        out_shape=jax.ShapeDtypeStruct(((TOP_K * t + 2 * tm) * ROW_TILE, HEAD_DIM), F32),
        compiler_params=_params(("arbitrary",), 56),
        name="routed_experts",
    )(tile_expert, n_used, meta, x_rows, w_gu, w_down)


def _dispatch_plan(eid, rank, counts, t):
    tm = EXPERT_TILE
    n_tiles = (TOP_K * t) // tm + N_EXPERTS
    n_slots = n_tiles * tm
    tok_bits = max((t - 1).bit_length(), 1)
    padded = ((counts + tm - 1) // tm) * tm
    ends = jnp.cumsum(padded)
    offs = ends - padded
    n_used = (ends[-1] // tm).astype(I32)
    tiles = jnp.arange(n_tiles, dtype=I32)
    experts = jnp.arange(N_EXPERTS, dtype=I32)
    tile_start = jnp.minimum(tiles, n_used - 1) * tm
    tile_expert = jnp.sum((ends[None, :] <= tile_start[:, None]).astype(I32), axis=1)
    tile_expert = jnp.clip(tile_expert, 0, N_EXPERTS - 1)
    of_tile = tile_expert[:, None] == experts[None, :]
    tile_count = jnp.sum(jnp.where(of_tile, counts[None, :], 0), axis=1)
    tile_offs = jnp.sum(jnp.where(of_tile, offs[None, :], 0), axis=1)
    tile_valid = jnp.clip(tile_count - (tiles * tm - tile_offs), 0, tm).astype(I32)
    slot = rank + jnp.sum(jnp.where(eid[:, :, None] == experts, offs, 0), axis=2)
    tok = lax.broadcasted_iota(I32, (TOP_K, t), 1)
    k = lax.broadcasted_iota(I32, (TOP_K, t), 0)
    packed = jnp.zeros((n_slots,), I32)
    packed = packed.at[slot.reshape(-1)].set((tok | (k << tok_bits)).reshape(-1), unique_indices=True)
    packed = packed.reshape(n_tiles, tm)
    row = lax.broadcasted_iota(I32, (n_tiles, tm), 1)
    gather_rows = packed & ((1 << tok_bits) - 1)
    real = (packed >> tok_bits) * t + gather_rows
    spare = TOP_K * t + (tiles[:, None] % 2) * tm + row
    scatter_rows = jnp.where(row < tile_valid[:, None], real, spare)
    plan = jnp.stack([gather_rows, scatter_rows], axis=1)
    dummy = jnp.stack([jnp.zeros((1, tm), I32), TOP_K * t + tm + row[:1]], axis=1)
    tail = jnp.concatenate([dummy, dummy], axis=0)
    meta = jnp.concatenate([dummy, plan, tail], axis=0).reshape(-1) * ROW_TILE
    return meta, tile_expert, n_used.reshape(1)


def _combine_kernel(x_ref, gate_ref, wgu_ref, wd_ref, g_ref, b_ref, *refs):
    y_refs = refs[:TOP_K]
    o_ref, ob_ref = refs[TOP_K:]
    ff = wd_ref.shape[0]
    x = x_ref[...]
    gu = _dot(x.astype(BF16), wgu_ref[...])
    a = gu[:, :ff]
    u = gu[:, ff:]
    hidden = (a * (1.0 / (1.0 + jnp.exp(-a))) * u).astype(BF16)
    ffn = _dot(hidden, wd_ref[...])
    gates = gate_ref[...]
    for k in range(TOP_K):
        ffn = ffn + gates[:, k:k + 1] * _load_rows(y_refs[k])
    out = _layer_norm(ALPHA * x + ffn, g_ref[...], b_ref[...])
    o_ref[...] = out
    ob_ref[...] = out.astype(BF16)


def _combine_norm(x1, gates_tk, y, w_gu_bf, w_down_bf, g, b):
    t, d = x1.shape
    ff = w_down_bf.shape[0]
    tm = min(128, t)
    nb = t // tm
    y_specs = [pl.BlockSpec((tm * ROW_TILE, HEAD_DIM), functools.partial(lambda i, k: (k * nb + i, 0), k=k))
               for k in range(TOP_K)]
    return pl.pallas_call(
        _combine_kernel,
        grid=(nb,),
        in_specs=[pl.BlockSpec((tm, d), lambda i: (i, 0)),
                  pl.BlockSpec((tm, TOP_K), lambda i: (i, 0)),
                  pl.BlockSpec((d, 2 * ff), lambda i: (0, 0)),
                  pl.BlockSpec((ff, d), lambda i: (0, 0)),
                  pl.BlockSpec((1, d), lambda i: (0, 0)),
                  pl.BlockSpec((1, d), lambda i: (0, 0))] + y_specs,
        out_specs=[pl.BlockSpec((tm, d), lambda i: (i, 0)),
                   pl.BlockSpec((tm, d), lambda i: (i, 0))],
        out_shape=[jax.ShapeDtypeStruct((t, d), F32),
                   jax.ShapeDtypeStruct((t, d), BF16)],
        compiler_params=_params(("arbitrary",), 48),
        name="combine_norm",
    )(x1, gates_tk, w_gu_bf, w_down_bf, g.reshape(1, d), b.reshape(1, d), *([y] * TOP_K))


def _moe_block(x1, x1_rows, router, router_bias, exp_w_gu, exp_w_down, shared_w_gu, shared_w_down,
               g, b):
    t = x1.shape[0]
    eid, rank, gate, counts = _router(x1, router, router_bias)
    meta, tile_expert, n_used = _dispatch_plan(eid, rank, counts[:, 0], t)
    y = _routed_experts(x1_rows, meta, tile_expert, n_used, exp_w_gu, exp_w_down)
    return _combine_norm(x1, gate.T, y, shared_w_gu.astype(BF16), shared_w_down.astype(BF16), g, b)


def _layer(x, xb, kv, mixer, w_in, forget_bias, w_o, ln_attn_g, ln_attn_b, router, router_bias,
           exp_w_gu, exp_w_down, shared_w_gu, shared_w_down, ln_ffn_g, ln_ffn_b):
    mix_w = MIX_HEADS * HEAD_DIM
    n_proj = 3 * mix_w + MEM_HEADS * HEAD_DIM
    proj = _in_proj(xb, w_in, n_proj, mix_w, 3 * mix_w)
    if mixer == 0:
        o_mix = _stick_attention(proj, MIX_HEADS)
    else:
        cum, cum_t = _forget_cumsum(x, w_in[:, n_proj:], forget_bias)
        o_mix = _fox_attention(proj, cum, cum_t, MIX_HEADS)
    o_mem = _memory_attention(proj, kv, 3 * mix_w)
    x1, x1_rows = _out_proj_norm(o_mix, o_mem, w_o.astype(BF16), x, ln_attn_g, ln_attn_b)
    return _moe_block(x1, x1_rows, router, router_bias, exp_w_gu, exp_w_down, shared_w_gu,
                      shared_w_down, ln_ffn_g, ln_ffn_b)


def kernel(x, mem, mem_ln_g, mem_ln_b, w_mem_kv,
           l0_w_in, l0_w_o, l0_ln_attn_g, l0_ln_attn_b, l0_router, l0_router_bias,
           l0_exp_w_gu, l0_exp_w_down, l0_shared_w_gu, l0_shared_w_down, l0_ln_ffn_g, l0_ln_ffn_b,
           l1_w_in, l1_forget_bias, l1_w_o, l1_ln_attn_g, l1_ln_attn_b, l1_router, l1_router_bias,
           l1_exp_w_gu, l1_exp_w_down, l1_shared_w_gu, l1_shared_w_down, l1_ln_ffn_g, l1_ln_ffn_b):
    batch, seq, d = x.shape
    assert batch == 1
    kv = _memory_kv(mem[0], mem_ln_g, mem_ln_b, w_mem_kv)
    x2 = x[0]
    x2, xb = _layer(x2, x2.astype(BF16), kv, 0, l0_w_in, None, l0_w_o, l0_ln_attn_g, l0_ln_attn_b,
                    l0_router, l0_router_bias, l0_exp_w_gu, l0_exp_w_down, l0_shared_w_gu,
                    l0_shared_w_down, l0_ln_ffn_g, l0_ln_ffn_b)
    x2, _ = _layer(x2, xb, kv, 1, l1_w_in, l1_forget_bias, l1_w_o, l1_ln_attn_g, l1_ln_attn_b,
                   l1_router, l1_router_bias, l1_exp_w_gu, l1_exp_w_down, l1_shared_w_gu,
                   l1_shared_w_down, l1_ln_ffn_g, l1_ln_ffn_b)
    return x2.reshape(batch, seq, d)
```

```python
import functools

import jax
import jax.numpy as jnp
from jax import lax
from jax.experimental import pallas as pl
from jax.experimental.pallas import tpu as pltpu

F32 = jnp.float32
BF16 = jnp.bfloat16
I32 = jnp.int32

HEAD_DIM = 128
MIX_HEADS = 12
MEM_HEADS = 4
N_EXPERTS = 64
TOP_K = 8
N_GROUPS = 8
GROUP_SIZE = N_EXPERTS // N_GROUPS
TOPK_GROUPS = 4
ROUTED_SCALE = 2.5
DEPTH = 2
ALPHA = float((2 * DEPTH) ** 0.25)
LN_EPS = 1e-5

MIB = 1024 * 1024
ATT_TILE = 256
FOX_TILE = 512
STICK_HEADS = 2
EXPERT_TILE = 256
META_RING = 4
ROW_TILE = 16
EXP_ZERO = -104.0
NEG_INF = float("-inf")


def _params(semantics, vmem_mib):
    return pltpu.CompilerParams(dimension_semantics=semantics, vmem_limit_bytes=vmem_mib * MIB)


def _dot(a, b):
    return jnp.dot(a, b, preferred_element_type=F32)


def _dot_nt(a, b):
    return lax.dot_general(a, b, (((1,), (1,)), ((), ())), preferred_element_type=F32)


def _split2(x):
    hi = x.astype(BF16)
    lo = (x - hi.astype(F32)).astype(BF16)
    return hi, lo


def _split3(x):
    hi = x.astype(BF16)
    r = x - hi.astype(F32)
    mid = r.astype(BF16)
    lo = (r - mid.astype(F32)).astype(BF16)
    return hi, mid, lo


def _store_rows(ref, stage, y, row_scale=None):
    rows = y.shape[0]
    for c in range(ROW_TILE):
        piece = y[:, c * HEAD_DIM:(c + 1) * HEAD_DIM]
        if row_scale is not None:
            piece = piece * row_scale
        stage[pl.ds(c, rows, stride=ROW_TILE), :] = piece
    ref[...] = stage[...].astype(ref.dtype)


def _load_rows(ref, stage):
    rows = ref.shape[0] // ROW_TILE
    stage[...] = ref[...].astype(F32)
    return jnp.concatenate([stage[pl.ds(c, rows, stride=ROW_TILE), :] for c in range(ROW_TILE)],
                           axis=1)


def _log_sigmoid(z):
    return jnp.minimum(z, 0.0) - jnp.log(1.0 + jnp.exp(-jnp.abs(z)))


def _layer_norm(y, g, b):
    mu = jnp.mean(y, axis=-1, keepdims=True)
    d = y - mu
    var = jnp.mean(d * d, axis=-1, keepdims=True)
    return d * lax.rsqrt(var + LN_EPS) * g + b


def _kv_kernel(mem_ref, g_ref, b_ref, w_ref, o_ref):
    y = _layer_norm(mem_ref[...], g_ref[...], b_ref[...])
    o_ref[...] = _dot(y.astype(BF16), w_ref[...].astype(BF16)).astype(o_ref.dtype)


def _memory_kv(mem, g, b, w):
    n, d = mem.shape
    nw = w.shape[1]
    tn = 512
    return pl.pallas_call(
        _kv_kernel,
        grid=(nw // tn,),
        in_specs=[pl.BlockSpec((n, d), lambda j: (0, 0)),
                  pl.BlockSpec((1, d), lambda j: (0, 0)),
                  pl.BlockSpec((1, d), lambda j: (0, 0)),
                  pl.BlockSpec((d, tn), lambda j: (0, j))],
        out_specs=pl.BlockSpec((n, tn), lambda j: (0, j)),
        out_shape=jax.ShapeDtypeStruct((n, nw), BF16),
        compiler_params=_params(("arbitrary",), 32),
        name="memory_kv",
    )(mem, g.reshape(1, d), b.reshape(1, d), w)


def _in_proj_kernel(x_ref, w_ref, o_ref, wbf_ref, *, k_col0, qmem_col0, scale):
    j = pl.program_id(0)
    tn = o_ref.shape[1]

    @pl.when(pl.program_id(1) == 0)
    def _():
        wbf_ref[...] = w_ref[...].astype(BF16)

    col0 = j * tn
    is_query = jnp.logical_or(col0 < k_col0, col0 >= qmem_col0)
    factor = jnp.where(is_query, scale, 1.0).astype(F32)
    o_ref[...] = (_dot(x_ref[...], wbf_ref[...]) * factor).astype(o_ref.dtype)


def _in_proj(xb, w, n_out, k_col0, qmem_col0):
    m, d = xb.shape
    tm, tn = 1024, 512
    tm = min(tm, m)
    assert k_col0 % tn == 0 and qmem_col0 % tn == 0
    kern = functools.partial(_in_proj_kernel, k_col0=k_col0, qmem_col0=qmem_col0,
                             scale=HEAD_DIM ** -0.5)
    return pl.pallas_call(
        kern,
        grid=(n_out // tn, m // tm),
        in_specs=[pl.BlockSpec((tm, d), lambda j, i: (i, 0)),
                  pl.BlockSpec((d, tn), lambda j, i: (0, j))],
        out_specs=pl.BlockSpec((tm, tn), lambda j, i: (i, j)),
        out_shape=jax.ShapeDtypeStruct((m, n_out), BF16),
        scratch_shapes=[pltpu.VMEM((d, tn), BF16)],
        compiler_params=_params(("arbitrary", "arbitrary"), 40),
        name="in_proj",
    )(xb, w)


def _gate_kernel(x_ref, w_ref, b_ref, cum_ref, cum_t_ref, carry_ref):
    tm = x_ref.shape[0]

    @pl.when(pl.program_id(0) == 0)
    def _():
        carry_ref[...] = jnp.zeros_like(carry_ref)

    xh, xl = _split2(x_ref[...])
    wh, wl = _split2(w_ref[...])
    f = _dot(xh, wh) + _dot(xh, wl) + _dot(xl, wh) + b_ref[...]
    lf = _log_sigmoid(f)
    row = lax.broadcasted_iota(I32, (tm, tm), 0)
    col = lax.broadcasted_iota(I32, (tm, tm), 1)
    lower = (col <= row).astype(BF16)
    p0, p1, p2 = _split3(lf)
    cum = _dot(lower, p0) + _dot(lower, p1) + _dot(lower, p2) + carry_ref[...]
    carry_ref[...] = cum[tm - 1:tm, :]
    cum_ref[...] = cum
    cum_t_ref[...] = cum.T[:cum_t_ref.shape[0], :]


def _forget_cumsum(x, w_f, bias):
    t, d = x.shape
    h = w_f.shape[1]
    tm = min(256, t)
    w_pad = jnp.zeros((d, HEAD_DIM), F32).at[:, :h].set(w_f)
    b_pad = jnp.zeros((1, HEAD_DIM), F32).at[0, :h].set(bias)
    return pl.pallas_call(
        _gate_kernel,
        grid=(t // tm,),
        in_specs=[pl.BlockSpec((tm, d), lambda i: (i, 0)),
                  pl.BlockSpec((d, HEAD_DIM), lambda i: (0, 0)),
                  pl.BlockSpec((1, HEAD_DIM), lambda i: (0, 0))],
        out_specs=[pl.BlockSpec((tm, HEAD_DIM), lambda i: (i, 0)),
                   pl.BlockSpec((16, tm), lambda i: (0, i))],
        out_shape=[jax.ShapeDtypeStruct((t, HEAD_DIM), F32),
                   jax.ShapeDtypeStruct((16, t), F32)],
        scratch_shapes=[pltpu.VMEM((1, HEAD_DIM), F32)],
        compiler_params=_params(("arbitrary",), 32),
        name="forget_cumsum",
    )(x, w_pad, b_pad)


def _stick_kernel(q_ref, k_ref, v_ref, o_ref):
    tq = q_ref.shape[0]
    i = pl.program_id(1)
    row = lax.broadcasted_iota(I32, (tq, tq), 0)
    col = lax.broadcasted_iota(I32, (tq, tq), 1)
    strict = col < row
    later = (row > col).astype(BF16)

    def block(j, c, acc, masked, head):
        start = pl.multiple_of(j * tq, tq)
        lanes = pl.ds(head * HEAD_DIM, HEAD_DIM)
        kb = k_ref[pl.ds(start, tq), lanes]
        vb = v_ref[pl.ds(start, tq), lanes]
        z = _dot_nt(q_ref[:, lanes], kb)
        soft =jnp.log(1.0 + jnp.exp(-jnp.abs(z)))
        log_beta = jnp.minimum(z, 0.0) - soft
        log_stay = jnp.minimum(-z, 0.0) - soft
        if masked:
            log_stay = jnp.where(strict, log_stay, 0.0)
        hi, lo = _split2(log_stay)
        log_after = _dot(hi, later) + _dot(lo, later) + c
        w = jnp.exp(log_beta + log_after)
        if masked:
            w = jnp.where(strict, w, 0.0)
        acc = acc + _dot(w.astype(BF16), vb)
        c = c + jnp.sum(log_stay, axis=1, keepdims=True)
        return c, acc

    heads = range(STICK_HEADS)
    c0 = jnp.zeros((tq, 1), F32)
    acc0 = jnp.zeros((tq, HEAD_DIM), F32)
    first = [block(i, c0, acc0, True, head) for head in heads]

    def any_alive(cs):
        worst = cs[0]
        for c in cs[1:]:
            worst = jnp.maximum(worst, c)
        return (jnp.max(worst) > EXP_ZERO).astype(I32)

    def cond(state):
        j, alive, _ = state
        return jnp.logical_and(j >= 0, alive > 0)

    def body(state):
        j, _, per_head = state
        per_head = [block(j, c, acc, False, head) for head, (c, acc) in zip(heads, per_head)]
        return j - 1, any_alive([c for c, _ in per_head]), per_head

    _, _, last = lax.while_loop(cond, body, (i - 1, any_alive([c for c, _ in first]), first))
    for head, (_, acc) in zip(heads, last):
        o_ref[:, pl.ds(head * HEAD_DIM, HEAD_DIM)] = acc.astype(o_ref.dtype)


def _stick_attention(proj, n_heads):
    t = proj.shape[0]
    tq = min(ATT_TILE, t)
    assert n_heads % STICK_HEADS == 0
    groups = n_heads // STICK_HEADS
    width = STICK_HEADS * HEAD_DIM
    return pl.pallas_call(
        _stick_kernel,
        grid=(groups, t // tq),
        in_specs=[pl.BlockSpec((tq, width), lambda h, i: (i, h)),
                  pl.BlockSpec((t, width), lambda h, i: (0, groups + h)),
                  pl.BlockSpec((t, width), lambda h, i: (0, 2 * groups + h))],
        out_specs=pl.BlockSpec((tq, width), lambda h, i: (i, h)),
        out_shape=jax.ShapeDtypeStruct((t, n_heads * HEAD_DIM), BF16),
        compiler_params=_params(("arbitrary", "arbitrary"), 40),
        name="stick_attention",
    )(proj, proj, proj)


def _fox_kernel(q_ref, k_ref, v_ref, fq_ref, fk_ref, o_ref, kmax_ref):
    tq = q_ref.shape[0]
    h = pl.program_id(0)
    i = pl.program_id(1)

    @pl.when(i == 0)
    def _():
        kk = k_ref[...].astype(F32)
        norm2 = jnp.max(jnp.sum(kk * kk, axis=1, keepdims=True), axis=0, keepdims=True)
        kmax_ref[...] = jnp.broadcast_to(jnp.sqrt(norm2), kmax_ref.shape)

    q = q_ref[...]
    lane = lax.broadcasted_iota(I32, fq_ref.shape, 1)
    fq = jnp.sum(jnp.where(lane == h, fq_ref[...], 0.0), axis=1, keepdims=True)
    qf = q.astype(F32)
    reach = jnp.sqrt(jnp.sum(qf * qf, axis=1, keepdims=True)) * kmax_ref[0:1, 0:1] + fq
    row = lax.broadcasted_iota(I32, (tq, tq), 0)
    col = lax.broadcasted_iota(I32, (tq, tq), 1)
    causal = col <= row

    def block(j, m, l, acc, masked):
        start = pl.multiple_of(j * tq, tq)
        kb = k_ref[pl.ds(start, tq), :]
        vb = v_ref[pl.ds(start, tq), :]
        fk = fk_ref[0, j]
        s = _dot_nt(q, kb) + (fq - fk)
        if masked:
            s = jnp.where(causal, s, NEG_INF)
        m_new = jnp.maximum(m, jnp.max(s, axis=1, keepdims=True))
        alpha = jnp.exp(m - m_new)
        p = jnp.exp(s - m_new)
        l = alpha * l + jnp.sum(p, axis=1, keepdims=True)
        acc = alpha * acc + _dot(p.astype(BF16), vb)
        return m_new, l, acc

    m0 = jnp.full((tq, 1), NEG_INF, F32)
    l0 = jnp.zeros((tq, 1), F32)
    acc0 = jnp.zeros((tq, HEAD_DIM), F32)
    m1, l1, acc1 = block(i, m0, l0, acc0, True)

    def alive_before(jb, m):
        last = fk_ref[0, jnp.maximum(jb - 1, 0)][:, tq - 1:tq]
        return (jnp.max(reach - last - m) > EXP_ZERO - 2.0).astype(I32)

    def cond(state):
        jb, alive, _, _, _ = state
        return jnp.logical_and(jb >= 0, alive > 0)

    def body(state):
        jb, _, m, l, acc = state
        m, l, acc = block(jb, m, l, acc, False)
        return jb - 1, alive_before(jb, m), m, l, acc

    _, _, _, l, acc = lax.while_loop(cond, body, (i - 1, alive_before(i, m1), m1, l1, acc1))
    o_ref[...] = (acc * (1.0 / l)).astype(o_ref.dtype)


def _fox_attention(proj, cum, cum_t, n_heads):
    t = proj.shape[0]
    tq = min(FOX_TILE, t)
    fk = cum_t.reshape(cum_t.shape[0], t // tq, 1, tq)
    return pl.pallas_call(
        _fox_kernel,
        grid=(n_heads, t // tq),
        in_specs=[pl.BlockSpec((tq, HEAD_DIM), lambda h, i: (i, h)),
                  pl.BlockSpec((t, HEAD_DIM), lambda h, i: (0, n_heads + h)),
                  pl.BlockSpec((t, HEAD_DIM), lambda h, i: (0, 2 * n_heads + h)),
                  pl.BlockSpec((tq, HEAD_DIM), lambda h, i: (i, 0)),
                  pl.BlockSpec((1, t // tq, 1, tq), lambda h, i: (h, 0, 0, 0))],
        out_specs=pl.BlockSpec((tq, HEAD_DIM), lambda h, i: (i, h)),
        out_shape=jax.ShapeDtypeStruct((t, n_heads * HEAD_DIM), BF16),
        scratch_shapes=[pltpu.VMEM((8, HEAD_DIM), F32)],
        compiler_params=_params(("arbitrary", "arbitrary"), 40),
        name="fox_attention",
    )(proj, proj, proj, cum, fk)


def _mem_attn_kernel(q_ref, k_ref, v_ref, o_ref):
    s = _dot_nt(q_ref[...], k_ref[...])
    m = jnp.max(s, axis=1, keepdims=True)
    p = jnp.exp(s - m)
    l = jnp.sum(p, axis=1, keepdims=True)
    o_ref[...] = (_dot(p.astype(BF16), v_ref[...]) * (1.0 / l)).astype(o_ref.dtype)


def _memory_attention(proj, kv, q_col0):
    t = proj.shape[0]
    n_mem = kv.shape[0]
    tq = min(1024, t)
    c0 = q_col0 // HEAD_DIM
    return pl.pallas_call(
        _mem_attn_kernel,
        grid=(MEM_HEADS, t // tq),
        in_specs=[pl.BlockSpec((tq, HEAD_DIM), lambda h, i: (i, c0 + h)),
                  pl.BlockSpec((n_mem, HEAD_DIM), lambda h, i: (0, h)),
                  pl.BlockSpec((n_mem, HEAD_DIM), lambda h, i: (0, MEM_HEADS + h))],
        out_specs=pl.BlockSpec((tq, HEAD_DIM), lambda h, i: (i, h)),
        out_shape=jax.ShapeDtypeStruct((t, MEM_HEADS * HEAD_DIM), BF16),
        compiler_params=_params(("arbitrary", "arbitrary"), 32),
        name="memory_attention",
    )(proj, kv, kv)


def _out_proj_kernel(om_ref, oc_ref, w_ref, x_ref, g_ref, b_ref, y_ref, yp_ref, stage_ref):
    n_mix = om_ref.shape[1]
    mix = _dot(om_ref[...], w_ref[:n_mix, :]) + _dot(oc_ref[...], w_ref[n_mix:, :])
    y = _layer_norm(ALPHA * x_ref[...] + mix, g_ref[...], b_ref[...])
    y_ref[...] = y
    _store_rows(yp_ref, stage_ref, y)


def _out_proj_norm(o_mix, o_mem, w_o_bf, x, g, b):
    t, d = x.shape
    tm = min(512, t)
    n_mix, n_mem = o_mix.shape[1], o_mem.shape[1]
    assert d == ROW_TILE * HEAD_DIM
    return pl.pallas_call(
        _out_proj_kernel,
        grid=(t // tm,),
        in_specs=[pl.BlockSpec((tm, n_mix), lambda i: (i, 0)),
                  pl.BlockSpec((tm, n_mem), lambda i: (i, 0)),
                  pl.BlockSpec((n_mix + n_mem, d), lambda i: (0, 0)),
                  pl.BlockSpec((tm, d), lambda i: (i, 0)),
                  pl.BlockSpec((1, d), lambda i: (0, 0)),
                  pl.BlockSpec((1, d), lambda i: (0, 0))],
        out_specs=[pl.BlockSpec((tm, d), lambda i: (i, 0)),
                   pl.BlockSpec((tm * ROW_TILE, HEAD_DIM), lambda i: (i, 0))],
        out_shape=[jax.ShapeDtypeStruct((t, d), F32),
                   jax.ShapeDtypeStruct((t * ROW_TILE, HEAD_DIM), BF16)],
        scratch_shapes=[pltpu.VMEM((tm * ROW_TILE, HEAD_DIM), F32)],
        compiler_params=_params(("arbitrary",), 48),
        name="out_proj_norm",
    )(o_mix, o_mem, w_o_bf, x, g.reshape(1, d), b.reshape(1, d))


def _router_kernel(x_ref, wh_ref, wl_ref, b_ref, eid_ref, rank_ref, gate_ref, cnt_ref, carry_ref):
    tm = x_ref.shape[0]

    @pl.when(pl.program_id(0) == 0)
    def _():
        carry_ref[...] = jnp.zeros_like(carry_ref)

    xh, xl = _split2(x_ref[...])
    wh, wl = wh_ref[...], wl_ref[...]
    logits = _dot_nt(wh, xh) + _dot_nt(wh, xl) + _dot_nt(wl, xh)
    scores = 1.0 / (1.0 + jnp.exp(-logits))
    sel = scores + b_ref[:, 0:1]

    group_score = []
    for g in range(N_GROUPS):
        v = sel[g * GROUP_SIZE:(g + 1) * GROUP_SIZE, :]
        m1 = jnp.max(v, axis=0, keepdims=True)
        is_max = v == m1
        n_max = jnp.sum(is_max.astype(F32), axis=0, keepdims=True)
        m2 = jnp.max(jnp.where(is_max, NEG_INF, v), axis=0, keepdims=True)
        group_score.append(m1 + jnp.where(n_max >= 2.0, m1, m2))
    masked = []
    for g in range(N_GROUPS):
        ahead = jnp.zeros((1, tm), F32)
        for g2 in range(N_GROUPS):
            if g2 == g:
                continue
            beats = group_score[g2] > group_score[g]
            if g2 < g:
                beats = jnp.logical_or(beats, group_score[g2] == group_score[g])
            ahead = ahead + beats.astype(F32)
        v = sel[g * GROUP_SIZE:(g + 1) * GROUP_SIZE, :]
        masked.append(jnp.where(ahead < float(TOPK_GROUPS), v, NEG_INF))
    msel = jnp.concatenate(masked, axis=0)

    e_idx = lax.broadcasted_iota(I32, (N_EXPERTS, tm), 0)
    ahead = jnp.zeros((N_EXPERTS, tm), F32)
    for e2 in range(N_EXPERTS):
        other = msel[e2:e2 + 1, :]
        beats = jnp.logical_or(other > msel, jnp.logical_and(other == msel, e_idx > e2))
        ahead = ahead + beats.astype(F32)
    chosen = ahead < float(TOP_K)
    chosen_f = chosen.astype(F32)

    w = jnp.where(chosen, scores, 0.0)
    gates = w / jnp.sum(w, axis=0, keepdims=True) * ROUTED_SCALE

    chosen_b = chosen_f.astype(BF16)
    r64 = lax.broadcasted_iota(I32, (N_EXPERTS, N_EXPERTS), 0)
    c64 = lax.broadcasted_iota(I32, (N_EXPERTS, N_EXPERTS), 1)
    choice = _dot((c64 < r64).astype(BF16), chosen_b)
    rt = lax.broadcasted_iota(I32, (tm, tm), 0)
    ct = lax.broadcasted_iota(I32, (tm, tm), 1)
    rank = _dot(chosen_b, (rt < ct).astype(BF16)) + carry_ref[:, 0:1]
    carry_ref[...] = carry_ref[...] + jnp.sum(chosen_f, axis=1, keepdims=True)
    cnt_ref[...] = carry_ref[...].astype(I32)

    e_f = e_idx.astype(F32)
    eids, ranks, gts = [], [], []
    for k in range(TOP_K):
        pick = jnp.logical_and(chosen, choice == float(k))
        eids.append(jnp.sum(jnp.where(pick, e_f, 0.0), axis=0, keepdims=True))
        ranks.append(jnp.sum(jnp.where(pick, rank, 0.0), axis=0, keepdims=True))
        gts.append(jnp.sum(jnp.where(pick, gates, 0.0), axis=0, keepdims=True))
    eid_ref[...] = jnp.concatenate(eids, axis=0).astype(I32)
    rank_ref[...] = jnp.concatenate(ranks, axis=0).astype(I32)
    gate_ref[...] = jnp.concatenate(gts, axis=0)


def _router(x1, w_router, router_bias):
    t, d = x1.shape
    tm = min(512, t)
    wt = w_router.T
    wh, wl = _split2(wt)
    bias = jnp.broadcast_to(router_bias.astype(F32)[:, None], (N_EXPERTS, HEAD_DIM))
    return pl.pallas_call(
        _router_kernel,
        grid=(t // tm,),
        in_specs=[pl.BlockSpec((tm, d), lambda i: (i, 0)),
                  pl.BlockSpec((N_EXPERTS, d), lambda i: (0, 0)),
                  pl.BlockSpec((N_EXPERTS, d), lambda i: (0, 0)),
                  pl.BlockSpec((N_EXPERTS, HEAD_DIM), lambda i: (0, 0))],
        out_specs=[pl.BlockSpec((TOP_K, tm), lambda i: (0, i)),
                   pl.BlockSpec((TOP_K, tm), lambda i: (0, i)),
                   pl.BlockSpec((TOP_K, tm), lambda i: (0, i)),
                   pl.BlockSpec((N_EXPERTS, HEAD_DIM), lambda i: (0, 0))],
        out_shape=[jax.ShapeDtypeStruct((TOP_K, t), I32),
                   jax.ShapeDtypeStruct((TOP_K, t), I32),
                   jax.ShapeDtypeStruct((TOP_K, t), F32),
                   jax.ShapeDtypeStruct((N_EXPERTS, HEAD_DIM), I32)],
        scratch_shapes=[pltpu.VMEM((N_EXPERTS, HEAD_DIM), F32)],
        compiler_params=_params(("arbitrary",), 40),
        name="router",
    )(x1, wh, wl, bias)


def _expert_kernel(te_ref, nu_ref, gate_ref, meta_hbm, x_hbm, wgu_ref, wd_ref, y_hbm,
                   meta_s, xa, xb, ya, yb, x_stage, y_stage, gate_rows, wgu_bf, wd_bf,
                   msem, gsem, ssem, *, n_tok):
    tm = xa.shape[0] // ROW_TILE
    ff = wd_ref.shape[0]
    j = pl.program_id(0)
    nu = nu_ref[0]

    def meta_copy(a):
        slot = a % META_RING
        return pltpu.make_async_copy(meta_hbm.at[pl.ds(a * 2 * tm, 2 * tm)],
                                     meta_s.at[pl.ds(slot * 2 * tm, 2 * tm)], msem.at[slot])

    def start_gathers(a, x_dst, sem):
        base = (a % META_RING) * 2 * tm
        for r in range(tm):
            src = pl.multiple_of(meta_s[base + r], ROW_TILE)
            pltpu.async_copy(x_hbm.at[pl.ds(src, ROW_TILE), :],
                             x_dst.at[pl.ds(r * ROW_TILE, ROW_TILE), :], sem, priority=0)

    def start_scatters(a, y_src, sem):
        base = (a % META_RING) * 2 * tm + tm
        for r in range(tm):
            dest = pl.multiple_of(meta_s[base + r], ROW_TILE)
            pltpu.async_copy(y_src.at[pl.ds(r * ROW_TILE, ROW_TILE), :],
                             y_hbm.at[pl.ds(dest, ROW_TILE), :], sem, priority=1)

    def wait_gathers(x_dst, sem):
        pltpu.make_async_copy(x_hbm.at[pl.ds(0, tm * ROW_TILE), :], x_dst, sem).wait()

    def wait_scatters(y_src, sem):
        pltpu.make_async_copy(y_src, y_hbm.at[pl.ds(0, tm * ROW_TILE), :], sem).wait()

    @pl.when(j == 0)
    def _():
        ya[...] = jnp.zeros_like(ya)
        yb[...] = jnp.zeros_like(yb)
        pltpu.make_async_copy(ya, y_hbm.at[pl.ds(TOP_K * n_tok * ROW_TILE, tm * ROW_TILE), :],
                              ssem.at[0]).start()
        meta_copy(0).start()
        meta_copy(1).start()
        meta_copy(2).start()
        meta_copy(0).wait()
        meta_copy(1).wait()
        start_gathers(1, xa, gsem.at[0])

    def load_gates(a):
        base = (a % META_RING) * 2 * tm + tm
        for r in range(tm):
            y_row = jnp.right_shift(meta_s[base + r], ROW_TILE.bit_length() - 1)
            gate = gate_ref[jnp.minimum(y_row, TOP_K * n_tok - 1)]
            gate_rows[pl.ds(r, 1), :] = jnp.full((1, HEAD_DIM), gate, F32)

    def step(x_cur, y_cur, x_nxt, y_prv, p):
        wait_gathers(x_cur, gsem.at[p])
        wait_scatters(y_cur, ssem.at[p])
        start_gathers(j + 2, x_nxt, gsem.at[1 - p])
        start_scatters(j, y_prv, ssem.at[1 - p])
        load_gates(j + 1)
        gu = _dot(_load_rows(x_cur, x_stage).astype(BF16), wgu_bf[...])
        a = gu[:, :ff]
        u = gu[:, ff:]
        hidden = (a * (1.0 / (1.0 + jnp.exp(-a))) * u).astype(BF16)
        _store_rows(y_cur, y_stage, _dot(hidden, wd_bf[...]), gate_rows[...])

        @pl.when(j == nu - 1)
        def _():
            start_scatters(j + 1, y_cur, ssem.at[p])
            wait_scatters(y_prv, ssem.at[1 - p])
            wait_scatters(y_cur, ssem.at[p])
            wait_gathers(x_nxt, gsem.at[1 - p])
            meta_copy(j + 3).wait()

    @pl.when(j < nu)
    def _():
        meta_copy(j + 3).start()
        meta_copy(j + 2).wait()
        new_expert = jnp.logical_or(j == 0, te_ref[j] != te_ref[jnp.maximum(j - 1, 0)])

        @pl.when(new_expert)
        def _():
            wgu_bf[...] = wgu_ref[...].astype(BF16)
            wd_bf[...] = wd_ref[...].astype(BF16)

        @pl.when(j % 2 == 0)
        def _():
            step(xa, ya, xb, yb, 0)

        @pl.when(j % 2 == 1)
        def _():
            step(xb, yb, xa, ya, 1)


def _routed_experts(x_rows, meta, tile_expert, n_used, gates, w_gu, w_down):
    t = x_rows.shape[0] // ROW_TILE
    d = w_gu.shape[1]
    ff = w_down.shape[1]
    tm = EXPERT_TILE
    n_tiles = tile_expert.shape[0]
    kern = functools.partial(_expert_kernel, n_tok=t)
    row_buf = pltpu.VMEM((tm * ROW_TILE, HEAD_DIM), BF16)
    stage_buf = pltpu.VMEM((tm * ROW_TILE, HEAD_DIM), F32)
    grid_spec = pltpu.PrefetchScalarGridSpec(
        num_scalar_prefetch=3,
        grid=(n_tiles,),
        in_specs=[pl.BlockSpec(memory_space=pl.ANY),
                  pl.BlockSpec(memory_space=pl.ANY),
                  pl.BlockSpec((None, d, 2 * ff), lambda j, te, nu, gt: (te[j], 0, 0)),
                  pl.BlockSpec((None, ff, d), lambda j, te, nu, gt: (te[j], 0, 0))],
        out_specs=pl.BlockSpec(memory_space=pl.ANY),
        scratch_shapes=[pltpu.SMEM((META_RING * 2 * tm,), I32),
                        row_buf, row_buf, row_buf, row_buf, stage_buf, stage_buf,
                        pltpu.VMEM((tm, HEAD_DIM), F32),
                        pltpu.VMEM((d, 2 * ff), BF16),
                        pltpu.VMEM((ff, d), BF16),
                        pltpu.SemaphoreType.DMA((META_RING,)),
                        pltpu.SemaphoreType.DMA((2,)),
                        pltpu.SemaphoreType.DMA((2,))],
    )
    return pl.pallas_call(
        kern,
        grid_spec=grid_spec,
        out_shape=jax.ShapeDtypeStruct(((TOP_K * t + 2 * tm) * ROW_TILE, HEAD_DIM), BF16),
        compiler_params=_params(("arbitrary",), 56),
        name="routed_experts",
    )(tile_expert, n_used, gates, meta, x_rows, w_gu, w_down)


def _dispatch_plan(eid, rank, counts, t):
    tm = EXPERT_TILE
    n_tiles = (TOP_K * t) // tm + N_EXPERTS
    n_slots = n_tiles * tm
    tok_bits = max((t - 1).bit_length(), 1)
    padded = ((counts + tm - 1) // tm) * tm
    ends = jnp.cumsum(padded)
    offs = ends - padded
    n_used = (ends[-1] // tm).astype(I32)
    tiles = jnp.arange(n_tiles, dtype=I32)
    experts = jnp.arange(N_EXPERTS, dtype=I32)
    tile_start = jnp.minimum(tiles, n_used - 1) * tm
    tile_expert = jnp.sum((ends[None, :] <= tile_start[:, None]).astype(I32), axis=1)
    tile_expert = jnp.clip(tile_expert, 0, N_EXPERTS - 1)
    of_tile = tile_expert[:, None] == experts[None, :]
    tile_count = jnp.sum(jnp.where(of_tile, counts[None, :], 0), axis=1)
    tile_offs = jnp.sum(jnp.where(of_tile, offs[None, :], 0), axis=1)
    tile_valid = jnp.clip(tile_count - (tiles * tm - tile_offs), 0, tm).astype(I32)
    slot = rank + jnp.sum(jnp.where(eid[:, :, None] == experts, offs, 0), axis=2)
    tok = lax.broadcasted_iota(I32, (TOP_K, t), 1)
    k = lax.broadcasted_iota(I32, (TOP_K, t), 0)
    packed = jnp.zeros((n_slots,), I32)
    packed = packed.at[slot.reshape(-1)].set((tok | (k << tok_bits)).reshape(-1), unique_indices=True)
    packed = packed.reshape(n_tiles, tm)
    row = lax.broadcasted_iota(I32, (n_tiles, tm), 1)
    gather_rows = packed & ((1 << tok_bits) - 1)
    real = (packed >> tok_bits) * t + gather_rows
    spare = TOP_K * t + (tiles[:, None] % 2) * tm + row
    scatter_rows = jnp.where(row < tile_valid[:, None], real, spare)
    plan = jnp.stack([gather_rows, scatter_rows], axis=1)
    dummy = jnp.stack([jnp.zeros((1, tm), I32), TOP_K * t + tm + row[:1]], axis=1)
    tail = jnp.concatenate([dummy, dummy], axis=0)
    meta = jnp.concatenate([dummy, plan, tail], axis=0).reshape(-1) * ROW_TILE
    return meta, tile_expert, n_used.reshape(1)


def _combine_kernel(x_ref, wgu_ref, wd_ref, g_ref, b_ref, *refs):
    y_refs = refs[:TOP_K]
    o_ref, ob_ref, stage_ref = refs[TOP_K:]
    tm = x_ref.shape[0]
    ff = wd_ref.shape[0]
    x = x_ref[...]
    gu = _dot(x.astype(BF16), wgu_ref[...])
    a = gu[:, :ff]
    u = gu[:, ff:]
    hidden = (a * (1.0 / (1.0 + jnp.exp(-a))) * u).astype(BF16)
    ffn = _dot(hidden, wd_ref[...])
    routed = y_refs[0][...].astype(F32)
    for k in range(1, TOP_K):
        routed = routed + y_refs[k][...].astype(F32)
    stage_ref[...] = routed
    routed = jnp.concatenate(
        [stage_ref[pl.ds(c, tm, stride=ROW_TILE), :] for c in range(ROW_TILE)], axis=1)
    out = _layer_norm(ALPHA * x + (ffn + routed), g_ref[...], b_ref[...])
    o_ref[...] = out
    ob_ref[...] = out.astype(BF16)


def _combine_norm(x1, y, w_gu_bf, w_down_bf, g, b):
    t, d = x1.shape
    ff = w_down_bf.shape[0]
    tm = min(256, t)
    nb = t // tm
    y_specs = [pl.BlockSpec((tm * ROW_TILE, HEAD_DIM), functools.partial(lambda i, k: (k * nb + i, 0), k=k))
               for k in range(TOP_K)]
    return pl.pallas_call(
        _combine_kernel,
        grid=(nb,),
        in_specs=[pl.BlockSpec((tm, d), lambda i: (i, 0)),
                  pl.BlockSpec((d, 2 * ff), lambda i: (0, 0)),
                  pl.BlockSpec((ff, d), lambda i: (0, 0)),
                  pl.BlockSpec((1, d), lambda i: (0, 0)),
                  pl.BlockSpec((1, d), lambda i: (0, 0))] + y_specs,
        out_specs=[pl.BlockSpec((tm, d), lambda i: (i, 0)),
                   pl.BlockSpec((tm, d), lambda i: (i, 0))],
        out_shape=[jax.ShapeDtypeStruct((t, d), F32),
                   jax.ShapeDtypeStruct((t, d), BF16)],
        scratch_shapes=[pltpu.VMEM((tm * ROW_TILE, HEAD_DIM), F32)],
        compiler_params=_params(("arbitrary",), 48),
        name="combine_norm",
    )(x1, w_gu_bf, w_down_bf, g.reshape(1, d), b.reshape(1, d), *([y] * TOP_K))


def _moe_block(x1, x1_rows, router, router_bias, exp_w_gu, exp_w_down, shared_w_gu, shared_w_down,
               g, b):
    t = x1.shape[0]
    eid, rank, gate, counts = _router(x1, router, router_bias)
    meta, tile_expert, n_used = _dispatch_plan(eid, rank, counts[:, 0], t)
    y = _routed_experts(x1_rows, meta, tile_expert, n_used, gate.reshape(-1), exp_w_gu, exp_w_down)
    return _combine_norm(x1, y, shared_w_gu.astype(BF16), shared_w_down.astype(BF16), g, b)


def _layer(x, xb, kv, mixer, w_in, forget_bias, w_o, ln_attn_g, ln_attn_b, router, router_bias,
           exp_w_gu, exp_w_down, shared_w_gu, shared_w_down, ln_ffn_g, ln_ffn_b):
    mix_w = MIX_HEADS * HEAD_DIM
    n_proj = 3 * mix_w + MEM_HEADS * HEAD_DIM
    proj = _in_proj(xb, w_in, n_proj, mix_w, 3 * mix_w)
    if mixer == 0:
        o_mix = _stick_attention(proj, MIX_HEADS)
    else:
        cum, cum_t = _forget_cumsum(x, w_in[:, n_proj:], forget_bias)
        o_mix = _fox_attention(proj, cum, cum_t, MIX_HEADS)
    o_mem = _memory_attention(proj, kv, 3 * mix_w)
    x1, x1_rows = _out_proj_norm(o_mix, o_mem, w_o.astype(BF16), x, ln_attn_g, ln_attn_b)
    return _moe_block(x1, x1_rows, router, router_bias, exp_w_gu, exp_w_down, shared_w_gu,
                      shared_w_down, ln_ffn_g, ln_ffn_b)


def kernel(x, mem, mem_ln_g, mem_ln_b, w_mem_kv,
           l0_w_in, l0_w_o, l0_ln_attn_g, l0_ln_attn_b, l0_router, l0_router_bias,
           l0_exp_w_gu, l0_exp_w_down, l0_shared_w_gu, l0_shared_w_down, l0_ln_ffn_g, l0_ln_ffn_b,
           l1_w_in, l1_forget_bias, l1_w_o, l1_ln_attn_g, l1_ln_attn_b, l1_router, l1_router_bias,
           l1_exp_w_gu, l1_exp_w_down, l1_shared_w_gu, l1_shared_w_down, l1_ln_ffn_g, l1_ln_ffn_b):
    batch, seq, d = x.shape
    assert batch == 1
    kv = _memory_kv(mem[0], mem_ln_g, mem_ln_b, w_mem_kv)
    x2 = x[0]
    x2, xb = _layer(x2, x2.astype(BF16), kv, 0, l0_w_in, None, l0_w_o, l0_ln_attn_g, l0_ln_attn_b,
                    l0_router, l0_router_bias, l0_exp_w_gu, l0_exp_w_down, l0_shared_w_gu,
                    l0_shared_w_down, l0_ln_ffn_g, l0_ln_ffn_b)
    x2, _ = _layer(x2, xb, kv, 1, l1_w_in, l1_forget_bias, l1_w_o, l1_ln_attn_g, l1_ln_attn_b,
                   l1_router, l1_router_bias, l1_exp_w_gu, l1_exp_w_down, l1_shared_w_gu,
                   l1_shared_w_down, l1_ln_ffn_g, l1_ln_ffn_b)
    return x2.reshape(batch, seq, d)
```

```python
import functools

import jax
import jax.numpy as jnp
from jax import lax
from jax.experimental import pallas as pl
from jax.experimental.pallas import tpu as pltpu

F32 = jnp.float32
BF16 = jnp.bfloat16
I32 = jnp.int32

HEAD_DIM = 128
MIX_HEADS = 12
MEM_HEADS = 4
N_EXPERTS = 64
TOP_K = 8
N_GROUPS = 8
GROUP_SIZE = N_EXPERTS // N_GROUPS
TOPK_GROUPS = 4
ROUTED_SCALE = 2.5
DEPTH = 2
ALPHA = float((2 * DEPTH) ** 0.25)
LN_EPS = 1e-5

MIB = 1024 * 1024
ATT_TILE = 256
FOX_TILE = 512
STICK_HEADS = 2
EXPERT_TILE = 256
META_RING = 4
ROW_TILE = 16
INVERT_UNROLL = 32
INVERT_STEPS = 64
EXP_ZERO = -104.0
NEG_INF = float("-inf")


def _params(semantics, vmem_mib):
    return pltpu.CompilerParams(dimension_semantics=semantics, vmem_limit_bytes=vmem_mib * MIB)


def _dot(a, b):
    return jnp.dot(a, b, preferred_element_type=F32)


def _dot_nt(a, b):
    return lax.dot_general(a, b, (((1,), (1,)), ((), ())), preferred_element_type=F32)


def _split2(x):
    hi = x.astype(BF16)
    lo = (x - hi.astype(F32)).astype(BF16)
    return hi, lo


def _split3(x):
    hi = x.astype(BF16)
    r = x - hi.astype(F32)
    mid = r.astype(BF16)
    lo = (r - mid.astype(F32)).astype(BF16)
    return hi, mid, lo


def _store_rows(ref, y):
    rows = y.shape[0]
    for c in range(ROW_TILE):
        ref[pl.ds(c, rows, stride=ROW_TILE), :] = y[:, c * HEAD_DIM:(c + 1) * HEAD_DIM]


def _load_rows(ref):
    rows = ref.shape[0] // ROW_TILE
    return jnp.concatenate([ref[pl.ds(c, rows, stride=ROW_TILE), :] for c in range(ROW_TILE)],
                           axis=1)


def _log_sigmoid(z):
    return jnp.minimum(z, 0.0) - jnp.log(1.0 + jnp.exp(-jnp.abs(z)))


def _layer_norm(y, g, b):
    mu = jnp.mean(y, axis=-1, keepdims=True)
    d = y - mu
    var = jnp.mean(d * d, axis=-1, keepdims=True)
    return d * lax.rsqrt(var + LN_EPS) * g + b


def _kv_kernel(mem_ref, g_ref, b_ref, w_ref, o_ref):
    y = _layer_norm(mem_ref[...], g_ref[...], b_ref[...])
    o_ref[...] = _dot(y.astype(BF16), w_ref[...].astype(BF16)).astype(o_ref.dtype)


def _memory_kv(mem, g, b, w):
    n, d = mem.shape
    nw = w.shape[1]
    tn = 512
    return pl.pallas_call(
        _kv_kernel,
        grid=(nw // tn,),
        in_specs=[pl.BlockSpec((n, d), lambda j: (0, 0)),
                  pl.BlockSpec((1, d), lambda j: (0, 0)),
                  pl.BlockSpec((1, d), lambda j: (0, 0)),
                  pl.BlockSpec((d, tn), lambda j: (0, j))],
        out_specs=pl.BlockSpec((n, tn), lambda j: (0, j)),
        out_shape=jax.ShapeDtypeStruct((n, nw), BF16),
        compiler_params=_params(("arbitrary",), 32),
        name="memory_kv",
    )(mem, g.reshape(1, d), b.reshape(1, d), w)


def _in_proj_kernel(x_ref, w_ref, o_ref, wbf_ref, *, k_col0, qmem_col0, scale):
    j = pl.program_id(0)
    tn = o_ref.shape[1]

    @pl.when(pl.program_id(1) == 0)
    def _():
        wbf_ref[...] = w_ref[...].astype(BF16)

    col0 = j * tn
    is_query = jnp.logical_or(col0 < k_col0, col0 >= qmem_col0)
    factor = jnp.where(is_query, scale, 1.0).astype(F32)
    o_ref[...] = (_dot(x_ref[...], wbf_ref[...]) * factor).astype(o_ref.dtype)


def _in_proj(xb, w, n_out, k_col0, qmem_col0):
    m, d = xb.shape
    tm, tn = 1024, 512
    tm = min(tm, m)
    assert k_col0 % tn == 0 and qmem_col0 % tn == 0
    kern = functools.partial(_in_proj_kernel, k_col0=k_col0, qmem_col0=qmem_col0,
                             scale=HEAD_DIM ** -0.5)
    return pl.pallas_call(
        kern,
        grid=(n_out // tn, m // tm),
        in_specs=[pl.BlockSpec((tm, d), lambda j, i: (i, 0)),
                  pl.BlockSpec((d, tn), lambda j, i: (0, j))],
        out_specs=pl.BlockSpec((tm, tn), lambda j, i: (i, j)),
        out_shape=jax.ShapeDtypeStruct((m, n_out), BF16),
        scratch_shapes=[pltpu.VMEM((d, tn), BF16)],
        compiler_params=_params(("arbitrary", "arbitrary"), 40),
        name="in_proj",
    )(xb, w)


def _gate_kernel(x_ref, w_ref, b_ref, cum_ref, cum_t_ref, carry_ref):
    tm = x_ref.shape[0]

    @pl.when(pl.program_id(0) == 0)
    def _():
        carry_ref[...] = jnp.zeros_like(carry_ref)

    xh, xl = _split2(x_ref[...])
    wh, wl = _split2(w_ref[...])
    f = _dot(xh, wh) + _dot(xh, wl) + _dot(xl, wh) + b_ref[...]
    lf = _log_sigmoid(f)
    row = lax.broadcasted_iota(I32, (tm, tm), 0)
    col = lax.broadcasted_iota(I32, (tm, tm), 1)
    lower = (col <= row).astype(BF16)
    p0, p1, p2 = _split3(lf)
    cum = _dot(lower, p0) + _dot(lower, p1) + _dot(lower, p2) + carry_ref[...]
    carry_ref[...] = cum[tm - 1:tm, :]
    cum_ref[...] = cum
    cum_t_ref[...] = cum.T[:cum_t_ref.shape[0], :]


def _forget_cumsum(x, w_f, bias):
    t, d = x.shape
    h = w_f.shape[1]
    tm = min(256, t)
    w_pad = jnp.zeros((d, HEAD_DIM), F32).at[:, :h].set(w_f)
    b_pad = jnp.zeros((1, HEAD_DIM), F32).at[0, :h].set(bias)
    return pl.pallas_call(
        _gate_kernel,
        grid=(t // tm,),
        in_specs=[pl.BlockSpec((tm, d), lambda i: (i, 0)),
                  pl.BlockSpec((d, HEAD_DIM), lambda i: (0, 0)),
                  pl.BlockSpec((1, HEAD_DIM), lambda i: (0, 0))],
        out_specs=[pl.BlockSpec((tm, HEAD_DIM), lambda i: (i, 0)),
                   pl.BlockSpec((16, tm), lambda i: (0, i))],
        out_shape=[jax.ShapeDtypeStruct((t, HEAD_DIM), F32),
                   jax.ShapeDtypeStruct((16, t), F32)],
        scratch_shapes=[pltpu.VMEM((1, HEAD_DIM), F32)],
        compiler_params=_params(("arbitrary",), 32),
        name="forget_cumsum",
    )(x, w_pad, b_pad)


def _stick_kernel(q_ref, k_ref, v_ref, o_ref):
    tq = q_ref.shape[0]
    i = pl.program_id(1)
    row = lax.broadcasted_iota(I32, (tq, tq), 0)
    col = lax.broadcasted_iota(I32, (tq, tq), 1)
    strict = col < row
    later = (row > col).astype(BF16)

    def block(j, c, acc, masked, head):
        start = pl.multiple_of(j * tq, tq)
        lanes = pl.ds(head * HEAD_DIM, HEAD_DIM)
        kb = k_ref[pl.ds(start, tq), lanes]
        vb = v_ref[pl.ds(start, tq), lanes]
        z = _dot_nt(q_ref[:, lanes], kb)
        soft = jnp.log(1.0 + jnp.exp(-jnp.abs(z)))
        log_beta = jnp.minimum(z, 0.0) - soft
        log_stay = jnp.minimum(-z, 0.0) - soft
        if masked:
            log_stay = jnp.where(strict, log_stay, 0.0)
        hi, lo = _split2(log_stay)
        log_after = _dot(hi, later) + _dot(lo, later) + c
        w = jnp.exp(log_beta + log_after)
        if masked:
            w = jnp.where(strict, w, 0.0)
        acc = acc + _dot(w.astype(BF16), vb)
        c = c + jnp.sum(log_stay, axis=1, keepdims=True)
        return c, acc

    heads = range(STICK_HEADS)
    c0 = jnp.zeros((tq, 1), F32)
    acc0 = jnp.zeros((tq, HEAD_DIM), F32)
    first = [block(i, c0, acc0, True, head) for head in heads]

    def any_alive(cs):
        worst = cs[0]
        for c in cs[1:]:
            worst = jnp.maximum(worst, c)
        return (jnp.max(worst) > EXP_ZERO).astype(I32)

    def cond(state):
        j, alive, _ = state
        return jnp.logical_and(j >= 0, alive > 0)

    def body(state):
        j, _, per_head = state
        per_head = [block(j, c, acc, False, head) for head, (c, acc) in zip(heads, per_head)]
        return j - 1, any_alive([c for c, _ in per_head]), per_head

    _, _, last = lax.while_loop(cond, body, (i - 1, any_alive([c for c, _ in first]), first))
    for head, (_, acc) in zip(heads, last):
        o_ref[:, pl.ds(head * HEAD_DIM, HEAD_DIM)] = acc.astype(o_ref.dtype)


def _stick_attention(proj, n_heads):
    t = proj.shape[0]
    tq = min(ATT_TILE, t)
    assert n_heads % STICK_HEADS == 0
    groups = n_heads // STICK_HEADS
    width = STICK_HEADS * HEAD_DIM
    return pl.pallas_call(
        _stick_kernel,
        grid=(groups, t // tq),
        in_specs=[pl.BlockSpec((tq, width), lambda h, i: (i, h)),
                  pl.BlockSpec((t, width), lambda h, i: (0, groups + h)),
                  pl.BlockSpec((t, width), lambda h, i: (0, 2 * groups + h))],
        out_specs=pl.BlockSpec((tq, width), lambda h, i: (i, h)),
        out_shape=jax.ShapeDtypeStruct((t, n_heads * HEAD_DIM), BF16),
        compiler_params=_params(("arbitrary", "arbitrary"), 40),
        name="stick_attention",
    )(proj, proj, proj)


def _fox_kernel(q_ref, k_ref, v_ref, fq_ref, fk_ref, o_ref, kmax_ref):
    tq = q_ref.shape[0]
    h = pl.program_id(0)
    i = pl.program_id(1)

    @pl.when(i == 0)
    def _():
        kk = k_ref[...].astype(F32)
        norm2 = jnp.max(jnp.sum(kk * kk, axis=1, keepdims=True), axis=0, keepdims=True)
        kmax_ref[...] = jnp.broadcast_to(jnp.sqrt(norm2), kmax_ref.shape)

    q = q_ref[...]
    lane = lax.broadcasted_iota(I32, fq_ref.shape, 1)
    fq = jnp.sum(jnp.where(lane == h, fq_ref[...], 0.0), axis=1, keepdims=True)
    qf = q.astype(F32)
    reach = jnp.sqrt(jnp.sum(qf * qf, axis=1, keepdims=True)) * kmax_ref[0:1, 0:1] + fq
    row = lax.broadcasted_iota(I32, (tq, tq), 0)
    col = lax.broadcasted_iota(I32, (tq, tq), 1)
    causal = col <= row

    def block(j, m, l, acc, masked):
        start = pl.multiple_of(j * tq, tq)
        kb = k_ref[pl.ds(start, tq), :]
        vb = v_ref[pl.ds(start, tq), :]
        fk = fk_ref[0, j]
        s = _dot_nt(q, kb) + (fq - fk)
        if masked:
            s = jnp.where(causal, s, NEG_INF)
        m_new = jnp.maximum(m, jnp.max(s, axis=1, keepdims=True))
        alpha = jnp.exp(m - m_new)
        p = jnp.exp(s - m_new)
        l = alpha * l + jnp.sum(p, axis=1, keepdims=True)
        acc = alpha * acc + _dot(p.astype(BF16), vb)
        return m_new, l, acc

    m0 = jnp.full((tq, 1), NEG_INF, F32)
    l0 = jnp.zeros((tq, 1), F32)
    acc0 = jnp.zeros((tq, HEAD_DIM), F32)
    m1, l1, acc1 = block(i, m0, l0, acc0, True)

    def alive_before(jb, m):
        last = fk_ref[0, jnp.maximum(jb - 1, 0)][:, tq - 1:tq]
        return (jnp.max(reach - last - m) > EXP_ZERO - 2.0).astype(I32)

    def cond(state):
        jb, alive, _, _, _ = state
        return jnp.logical_and(jb >= 0, alive > 0)

    def body(state):
        jb, _, m, l, acc = state
        m, l, acc = block(jb, m, l, acc, False)
        return jb - 1, alive_before(jb, m), m, l, acc

    _, _, _, l, acc = lax.while_loop(cond, body, (i - 1, alive_before(i, m1), m1, l1, acc1))
    o_ref[...] = (acc * (1.0 / l)).astype(o_ref.dtype)


def _fox_attention(proj, cum, cum_t, n_heads):
    t = proj.shape[0]
    tq = min(FOX_TILE, t)
    fk = cum_t.reshape(cum_t.shape[0], t // tq, 1, tq)
    return pl.pallas_call(
        _fox_kernel,
        grid=(n_heads, t // tq),
        in_specs=[pl.BlockSpec((tq, HEAD_DIM), lambda h, i: (i, h)),
                  pl.BlockSpec((t, HEAD_DIM), lambda h, i: (0, n_heads + h)),
                  pl.BlockSpec((t, HEAD_DIM), lambda h, i: (0, 2 * n_heads + h)),
                  pl.BlockSpec((tq, HEAD_DIM), lambda h, i: (i, 0)),
                  pl.BlockSpec((1, t // tq, 1, tq), lambda h, i: (h, 0, 0, 0))],
        out_specs=pl.BlockSpec((tq, HEAD_DIM), lambda h, i: (i, h)),
        out_shape=jax.ShapeDtypeStruct((t, n_heads * HEAD_DIM), BF16),
        scratch_shapes=[pltpu.VMEM((8, HEAD_DIM), F32)],
        compiler_params=_params(("arbitrary", "arbitrary"), 40),
        name="fox_attention",
    )(proj, proj, proj, cum, fk)


def _mem_attn_kernel(q_ref, k_ref, v_ref, o_ref):
    s = _dot_nt(q_ref[...], k_ref[...])
    m = jnp.max(s, axis=1, keepdims=True)
    p = jnp.exp(s - m)
    l = jnp.sum(p, axis=1, keepdims=True)
    o_ref[...] = (_dot(p.astype(BF16), v_ref[...]) * (1.0 / l)).astype(o_ref.dtype)


def _memory_attention(proj, kv, q_col0):
    t = proj.shape[0]
    n_mem = kv.shape[0]
    tq = min(1024, t)
    c0 = q_col0 // HEAD_DIM
    return pl.pallas_call(
        _mem_attn_kernel,
        grid=(MEM_HEADS, t // tq),
        in_specs=[pl.BlockSpec((tq, HEAD_DIM), lambda h, i: (i, c0 + h)),
                  pl.BlockSpec((n_mem, HEAD_DIM), lambda h, i: (0, h)),
                  pl.BlockSpec((n_mem, HEAD_DIM), lambda h, i: (0, MEM_HEADS + h))],
        out_specs=pl.BlockSpec((tq, HEAD_DIM), lambda h, i: (i, h)),
        out_shape=jax.ShapeDtypeStruct((t, MEM_HEADS * HEAD_DIM), BF16),
        compiler_params=_params(("arbitrary", "arbitrary"), 32),
        name="memory_attention",
    )(proj, kv, kv)


def _out_proj_kernel(om_ref, oc_ref, w_ref, x_ref, g_ref, b_ref, y_ref, yp_ref):
    n_mix = om_ref.shape[1]
    mix = _dot(om_ref[...], w_ref[:n_mix, :]) + _dot(oc_ref[...], w_ref[n_mix:, :])
    y = _layer_norm(ALPHA * x_ref[...] + mix, g_ref[...], b_ref[...])
    y_ref[...] = y
    _store_rows(yp_ref, y)


def _out_proj_norm(o_mix, o_mem, w_o_bf, x, g, b):
    t, d = x.shape
    tm = min(512, t)
    n_mix, n_mem = o_mix.shape[1], o_mem.shape[1]
    assert d == ROW_TILE * HEAD_DIM
    return pl.pallas_call(
        _out_proj_kernel,
        grid=(t // tm,),
        in_specs=[pl.BlockSpec((tm, n_mix), lambda i: (i, 0)),
                  pl.BlockSpec((tm, n_mem), lambda i: (i, 0)),
                  pl.BlockSpec((n_mix + n_mem, d), lambda i: (0, 0)),
                  pl.BlockSpec((tm, d), lambda i: (i, 0)),
                  pl.BlockSpec((1, d), lambda i: (0, 0)),
                  pl.BlockSpec((1, d), lambda i: (0, 0))],
        out_specs=[pl.BlockSpec((tm, d), lambda i: (i, 0)),
                   pl.BlockSpec((tm * ROW_TILE, HEAD_DIM), lambda i: (i, 0))],
        out_shape=[jax.ShapeDtypeStruct((t, d), F32),
                   jax.ShapeDtypeStruct((t * ROW_TILE, HEAD_DIM), F32)],
        compiler_params=_params(("arbitrary",), 48),
        name="out_proj_norm",
    )(o_mix, o_mem, w_o_bf, x, g.reshape(1, d), b.reshape(1, d))


def _router_kernel(x_ref, wh_ref, wl_ref, b_ref, eid_ref, rank_ref, gate_ref, cnt_ref, carry_ref):
    tm = x_ref.shape[0]

    @pl.when(pl.program_id(0) == 0)
    def _():
        carry_ref[...] = jnp.zeros_like(carry_ref)

    xh, xl = _split2(x_ref[...])
    wh, wl = wh_ref[...], wl_ref[...]
    logits = _dot_nt(wh, xh) + _dot_nt(wh, xl) + _dot_nt(wl, xh)
    scores = 1.0 / (1.0 + jnp.exp(-logits))
    sel = scores + b_ref[:, 0:1]

    group_score = []
    for g in range(N_GROUPS):
        v = sel[g * GROUP_SIZE:(g + 1) * GROUP_SIZE, :]
        m1 = jnp.max(v, axis=0, keepdims=True)
        is_max = v == m1
        n_max = jnp.sum(is_max.astype(F32), axis=0, keepdims=True)
        m2 = jnp.max(jnp.where(is_max, NEG_INF, v), axis=0, keepdims=True)
        group_score.append(m1 + jnp.where(n_max >= 2.0, m1, m2))
    masked = []
    for g in range(N_GROUPS):
        ahead = jnp.zeros((1, tm), F32)
        for g2 in range(N_GROUPS):
            if g2 == g:
                continue
            beats = group_score[g2] > group_score[g]
            if g2 < g:
                beats = jnp.logical_or(beats, group_score[g2] == group_score[g])
            ahead = ahead + beats.astype(F32)
        v = sel[g * GROUP_SIZE:(g + 1) * GROUP_SIZE, :]
        masked.append(jnp.where(ahead < float(TOPK_GROUPS), v, NEG_INF))
    msel = jnp.concatenate(masked, axis=0)

    e_idx = lax.broadcasted_iota(I32, (N_EXPERTS, tm), 0)
    ahead = jnp.zeros((N_EXPERTS, tm), F32)
    for e2 in range(N_EXPERTS):
        other = msel[e2:e2 + 1, :]
        beats = jnp.logical_or(other > msel, jnp.logical_and(other == msel, e_idx > e2))
        ahead = ahead + beats.astype(F32)
    chosen = ahead < float(TOP_K)
    chosen_f = chosen.astype(F32)

    w = jnp.where(chosen, scores, 0.0)
    gates = w / jnp.sum(w, axis=0, keepdims=True) * ROUTED_SCALE

    chosen_b = chosen_f.astype(BF16)
    r64 = lax.broadcasted_iota(I32, (N_EXPERTS, N_EXPERTS), 0)
    c64 = lax.broadcasted_iota(I32, (N_EXPERTS, N_EXPERTS), 1)
    choice = _dot((c64 < r64).astype(BF16), chosen_b)
    rt = lax.broadcasted_iota(I32, (tm, tm), 0)
    ct = lax.broadcasted_iota(I32, (tm, tm), 1)
    rank = _dot(chosen_b, (rt < ct).astype(BF16)) + carry_ref[:, 0:1]
    carry_ref[...] = carry_ref[...] + jnp.sum(chosen_f, axis=1, keepdims=True)
    cnt_ref[...] = carry_ref[...].astype(I32)

    e_f = e_idx.astype(F32)
    eids, ranks, gts = [], [], []
    for k in range(TOP_K):
        pick = jnp.logical_and(chosen, choice == float(k))
        eids.append(jnp.sum(jnp.where(pick, e_f, 0.0), axis=0, keepdims=True))
        ranks.append(jnp.sum(jnp.where(pick, rank, 0.0), axis=0, keepdims=True))
        gts.append(jnp.sum(jnp.where(pick, gates, 0.0), axis=0, keepdims=True))
    eid_ref[...] = jnp.concatenate(eids, axis=0).astype(I32)
    rank_ref[...] = jnp.concatenate(ranks, axis=0).astype(I32)
    gate_ref[...] = jnp.concatenate(gts, axis=0)


def _router(x1, w_router, router_bias):
    t, d = x1.shape
    tm = min(512, t)
    wt = w_router.T
    wh, wl = _split2(wt)
    bias = jnp.broadcast_to(router_bias.astype(F32)[:, None], (N_EXPERTS, HEAD_DIM))
    return pl.pallas_call(
        _router_kernel,
        grid=(t // tm,),
        in_specs=[pl.BlockSpec((tm, d), lambda i: (i, 0)),
                  pl.BlockSpec((N_EXPERTS, d), lambda i: (0, 0)),
                  pl.BlockSpec((N_EXPERTS, d), lambda i: (0, 0)),
                  pl.BlockSpec((N_EXPERTS, HEAD_DIM), lambda i: (0, 0))],
        out_specs=[pl.BlockSpec((TOP_K, tm), lambda i: (0, i)),
                   pl.BlockSpec((TOP_K, tm), lambda i: (0, i)),
                   pl.BlockSpec((TOP_K, tm), lambda i: (0, i)),
                   pl.BlockSpec((N_EXPERTS, HEAD_DIM), lambda i: (0, 0))],
        out_shape=[jax.ShapeDtypeStruct((TOP_K, t), I32),
                   jax.ShapeDtypeStruct((TOP_K, t), I32),
                   jax.ShapeDtypeStruct((TOP_K, t), F32),
                   jax.ShapeDtypeStruct((N_EXPERTS, HEAD_DIM), I32)],
        scratch_shapes=[pltpu.VMEM((N_EXPERTS, HEAD_DIM), F32)],
        compiler_params=_params(("arbitrary",), 40),
        name="router",
    )(x1, wh, wl, bias)


def _expert_kernel(te_ref, nu_ref, meta_hbm, x_hbm, wgu_ref, wd_ref, y_hbm,
                   meta_s, xa, xb, ya, yb, wgu_bf, wd_bf, msem, gsem, ssem, *, n_tok):
    tm = xa.shape[0] // ROW_TILE
    ff = wd_ref.shape[0]
    j = pl.program_id(0)
    nu = nu_ref[0]

    def meta_copy(a):
        slot = a % META_RING
        return pltpu.make_async_copy(meta_hbm.at[pl.ds(a * 2 * tm, 2 * tm)],
                                     meta_s.at[pl.ds(slot * 2 * tm, 2 * tm)], msem.at[slot])

    def start_gathers(a, x_dst, sem):
        base = (a % META_RING) * 2 * tm
        for r in range(tm):
            src = pl.multiple_of(meta_s[base + r], ROW_TILE)
            pltpu.async_copy(x_hbm.at[pl.ds(src, ROW_TILE), :],
                             x_dst.at[pl.ds(r * ROW_TILE, ROW_TILE), :], sem, priority=r % 2)

    def start_scatters(a, y_src, sem):
        base = (a % META_RING) * 2 * tm + tm
        for r in range(tm):
            dest = pl.multiple_of(meta_s[base + r], ROW_TILE)
            pltpu.async_copy(y_src.at[pl.ds(r * ROW_TILE, ROW_TILE), :],
                             y_hbm.at[pl.ds(dest, ROW_TILE), :], sem, priority=r % 2)

    def wait_gathers(x_dst, sem):
        pltpu.make_async_copy(x_hbm.at[pl.ds(0, tm * ROW_TILE), :], x_dst, sem).wait()

    def wait_scatters(y_src, sem):
        pltpu.make_async_copy(y_src, y_hbm.at[pl.ds(0, tm * ROW_TILE), :], sem).wait()

    @pl.when(j == 0)
    def _():
        ya[...] = jnp.zeros_like(ya)
        yb[...] = jnp.zeros_like(yb)
        pltpu.make_async_copy(ya, y_hbm.at[pl.ds(TOP_K * n_tok * ROW_TILE, tm * ROW_TILE), :],
                              ssem.at[0]).start()
        meta_copy(0).start()
        meta_copy(1).start()
        meta_copy(2).start()
        meta_copy(0).wait()
        meta_copy(1).wait()
        start_gathers(1, xa, gsem.at[0])

    def step(x_cur, y_cur, x_nxt, y_prv, p):
        wait_gathers(x_cur, gsem.at[p])
        wait_scatters(y_cur, ssem.at[p])
        start_gathers(j + 2, x_nxt, gsem.at[1 - p])
        start_scatters(j, y_prv, ssem.at[1 - p])
        gu = _dot(_load_rows(x_cur).astype(BF16), wgu_bf[...])
        a = gu[:, :ff]
        u = gu[:, ff:]
        hidden = (a * (1.0 / (1.0 + jnp.exp(-a))) * u).astype(BF16)
        _store_rows(y_cur, _dot(hidden, wd_bf[...]))

        @pl.when(j == nu - 1)
        def _():
            start_scatters(j + 1, y_cur, ssem.at[p])
            wait_scatters(y_prv, ssem.at[1 - p])
            wait_scatters(y_cur, ssem.at[p])
            wait_gathers(x_nxt, gsem.at[1 - p])
            meta_copy(j + 3).wait()

    @pl.when(j < nu)
    def _():
        meta_copy(j + 3).start()
        meta_copy(j + 2).wait()
        new_expert = jnp.logical_or(j == 0, te_ref[j] != te_ref[jnp.maximum(j - 1, 0)])

        @pl.when(new_expert)
        def _():
            wgu_bf[...] = wgu_ref[...].astype(BF16)
            wd_bf[...] = wd_ref[...].astype(BF16)

        @pl.when(j % 2 == 0)
        def _():
            step(xa, ya, xb, yb, 0)

        @pl.when(j % 2 == 1)
        def _():
            step(xb, yb, xa, ya, 1)


def _routed_experts(x_rows, meta, tile_expert, n_used, w_gu, w_down):
    t = x_rows.shape[0] // ROW_TILE
    d = w_gu.shape[1]
    ff = w_down.shape[1]
    tm = EXPERT_TILE
    n_tiles = tile_expert.shape[0]
    kern = functools.partial(_expert_kernel, n_tok=t)
    row_buf = pltpu.VMEM((tm * ROW_TILE, HEAD_DIM), F32)
    grid_spec = pltpu.PrefetchScalarGridSpec(
        num_scalar_prefetch=2,
        grid=(n_tiles,),
        in_specs=[pl.BlockSpec(memory_space=pl.ANY),
                  pl.BlockSpec(memory_space=pl.ANY),
                  pl.BlockSpec((None, d, 2 * ff), lambda j, te, nu: (te[j], 0, 0)),
                  pl.BlockSpec((None, ff, d), lambda j, te, nu: (te[j], 0, 0))],
        out_specs=pl.BlockSpec(memory_space=pl.ANY),
        scratch_shapes=[pltpu.SMEM((META_RING * 2 * tm,), I32),
                        row_buf, row_buf, row_buf, row_buf,
                        pltpu.VMEM((d, 2 * ff), BF16),
                        pltpu.VMEM((ff, d), BF16),
                        pltpu.SemaphoreType.DMA((META_RING,)),
                        pltpu.SemaphoreType.DMA((2,)),
                        pltpu.SemaphoreType.DMA((2,))],
    )
    return pl.pallas_call(
        kern,
        grid_spec=grid_spec,
        out_shape=jax.ShapeDtypeStruct(((TOP_K * t + 2 * tm) * ROW_TILE, HEAD_DIM), F32),
        compiler_params=_params(("arbitrary",), 56),
        name="routed_experts",
    )(tile_expert, n_used, meta, x_rows, w_gu, w_down)


def _invert_kernel(slot_ref, out_ref):
    unroll = INVERT_UNROLL
    step = pl.program_id(0)
    n_clear = out_ref.shape[0] // INVERT_STEPS
    n_place = slot_ref.shape[0] // INVERT_STEPS

    @pl.when(step < INVERT_STEPS)
    def _():
        base = step * n_clear

        def clear(i, _):
            for u in range(unroll):
                out_ref[base + i * unroll + u] = 0
            return 0
        lax.fori_loop(0, n_clear // unroll, clear, 0)

    @pl.when(step >= INVERT_STEPS)
    def _():
        base = (step - INVERT_STEPS) * n_place

        def place(i, _):
            for u in range(unroll):
                src = base + i * unroll + u
                out_ref[slot_ref[src]] = src
            return 0
        lax.fori_loop(0, n_place // unroll, place, 0)


def _invert_slots(slot, n_slots):
    chunk = INVERT_STEPS * INVERT_UNROLL
    assert slot.shape[0] % chunk == 0 and n_slots % chunk == 0
    return pl.pallas_call(
        _invert_kernel,
        grid=(2 * INVERT_STEPS,),
        in_specs=[pl.BlockSpec(memory_space=pltpu.SMEM)],
        out_specs=pl.BlockSpec(memory_space=pltpu.SMEM),
        out_shape=jax.ShapeDtypeStruct((n_slots,), I32),
        compiler_params=pltpu.CompilerParams(dimension_semantics=("arbitrary",)),
        name="invert_slots",
    )(slot)


def _dispatch_plan(eid, rank, counts, t):
    tm = EXPERT_TILE
    n_tiles = (TOP_K * t) // tm + N_EXPERTS
    n_slots = n_tiles * tm
    assert t & (t - 1) == 0
    padded = ((counts + tm - 1) // tm) * tm
    ends = jnp.cumsum(padded)
    offs = ends - padded
    n_used = (ends[-1] // tm).astype(I32)
    tiles = jnp.arange(n_tiles, dtype=I32)
    experts = jnp.arange(N_EXPERTS, dtype=I32)
    tile_start = jnp.minimum(tiles, n_used - 1) * tm
    tile_expert = jnp.sum((ends[None, :] <= tile_start[:, None]).astype(I32), axis=1)
    tile_expert = jnp.clip(tile_expert, 0, N_EXPERTS - 1)
    of_tile = tile_expert[:, None] == experts[None, :]
    tile_count = jnp.sum(jnp.where(of_tile, counts[None, :], 0), axis=1)
    tile_offs = jnp.sum(jnp.where(of_tile, offs[None, :], 0), axis=1)
    tile_valid = jnp.clip(tile_count - (tiles * tm - tile_offs), 0, tm).astype(I32)
    slot = rank + jnp.sum(jnp.where(eid[:, :, None] == experts, offs, 0), axis=2)
    packed = _invert_slots(slot.reshape(-1), n_slots).reshape(n_tiles, tm)
    row = lax.broadcasted_iota(I32, (n_tiles, tm), 1)
    gather_rows = packed & (t - 1)
    spare = TOP_K * t + (tiles[:, None] % 2) * tm + row
    scatter_rows = jnp.where(row < tile_valid[:, None], packed, spare)
    plan = jnp.stack([gather_rows, scatter_rows], axis=1)
    dummy = jnp.stack([jnp.zeros((1, tm), I32), TOP_K * t + tm + row[:1]], axis=1)
    tail = jnp.concatenate([dummy, dummy], axis=0)
    meta = jnp.concatenate([dummy, plan, tail], axis=0).reshape(-1) * ROW_TILE
    return meta, tile_expert, n_used.reshape(1)


def _combine_kernel(x_ref, gate_ref, wgu_ref, wd_ref, g_ref, b_ref, *refs):
    y_refs = refs[:TOP_K]
    o_ref, ob_ref = refs[TOP_K:]
    ff = wd_ref.shape[0]
    x = x_ref[...]
    gu = _dot(x.astype(BF16), wgu_ref[...])
    a = gu[:, :ff]
    u = gu[:, ff:]
    hidden = (a * (1.0 / (1.0 + jnp.exp(-a))) * u).astype(BF16)
    ffn = _dot(hidden, wd_ref[...])
    gates = gate_ref[...]
    for k in range(TOP_K):
        ffn = ffn + gates[:, k:k + 1] * _load_rows(y_refs[k])
    out = _layer_norm(ALPHA * x + ffn, g_ref[...], b_ref[...])
    o_ref[...] = out
    ob_ref[...] = out.astype(BF16)


def _combine_norm(x1, gates_tk, y, w_gu_bf, w_down_bf, g, b):
    t, d = x1.shape
    ff = w_down_bf.shape[0]
    tm = min(128, t)
    nb = t // tm
    y_specs = [pl.BlockSpec((tm * ROW_TILE, HEAD_DIM), functools.partial(lambda i, k: (k * nb + i, 0), k=k))
               for k in range(TOP_K)]
    return pl.pallas_call(
        _combine_kernel,
        grid=(nb,),
        in_specs=[pl.BlockSpec((tm, d), lambda i: (i, 0)),
                  pl.BlockSpec((tm, TOP_K), lambda i: (i, 0)),
                  pl.BlockSpec((d, 2 * ff), lambda i: (0, 0)),
                  pl.BlockSpec((ff, d), lambda i: (0, 0)),
                  pl.BlockSpec((1, d), lambda i: (0, 0)),
                  pl.BlockSpec((1, d), lambda i: (0, 0))] + y_specs,
        out_specs=[pl.BlockSpec((tm, d), lambda i: (i, 0)),
                   pl.BlockSpec((tm, d), lambda i: (i, 0))],
        out_shape=[jax.ShapeDtypeStruct((t, d), F32),
                   jax.ShapeDtypeStruct((t, d), BF16)],
        compiler_params=_params(("arbitrary",), 48),
        name="combine_norm",
    )(x1, gates_tk, w_gu_bf, w_down_bf, g.reshape(1, d), b.reshape(1, d), *([y] * TOP_K))


def _moe_block(x1, x1_rows, router, router_bias, exp_w_gu, exp_w_down, shared_w_gu, shared_w_down,
               g, b):
    t = x1.shape[0]
    eid, rank, gate, counts = _router(x1, router, router_bias)
    meta, tile_expert, n_used = _dispatch_plan(eid, rank, counts[:, 0], t)
    y = _routed_experts(x1_rows, meta, tile_expert, n_used, exp_w_gu, exp_w_down)
    return _combine_norm(x1, gate.T, y, shared_w_gu.astype(BF16), shared_w_down.astype(BF16), g, b)


def _layer(x, xb, kv, mixer, w_in, forget_bias, w_o, ln_attn_g, ln_attn_b, router, router_bias,
           exp_w_gu, exp_w_down, shared_w_gu, shared_w_down, ln_ffn_g, ln_ffn_b):
    mix_w = MIX_HEADS * HEAD_DIM
    n_proj = 3 * mix_w + MEM_HEADS * HEAD_DIM
    proj = _in_proj(xb, w_in, n_proj, mix_w, 3 * mix_w)
    if mixer == 0:
        o_mix = _stick_attention(proj, MIX_HEADS)
    else:
        cum, cum_t = _forget_cumsum(x, w_in[:, n_proj:], forget_bias)
        o_mix = _fox_attention(proj, cum, cum_t, MIX_HEADS)
    o_mem = _memory_attention(proj, kv, 3 * mix_w)
    x1, x1_rows = _out_proj_norm(o_mix, o_mem, w_o.astype(BF16), x, ln_attn_g, ln_attn_b)
    return _moe_block(x1, x1_rows, router, router_bias, exp_w_gu, exp_w_down, shared_w_gu,
                      shared_w_down, ln_ffn_g, ln_ffn_b)


def kernel(x, mem, mem_ln_g, mem_ln_b, w_mem_kv,
           l0_w_in, l0_w_o, l0_ln_attn_g, l0_ln_attn_b, l0_router, l0_router_bias,
           l0_exp_w_gu, l0_exp_w_down, l0_shared_w_gu, l0_shared_w_down, l0_ln_ffn_g, l0_ln_ffn_b,
           l1_w_in, l1_forget_bias, l1_w_o, l1_ln_attn_g, l1_ln_attn_b, l1_router, l1_router_bias,
           l1_exp_w_gu, l1_exp_w_down, l1_shared_w_gu, l1_shared_w_down, l1_ln_ffn_g, l1_ln_ffn_b):
    batch, seq, d = x.shape
    assert batch == 1
    kv = _memory_kv(mem[0], mem_ln_g, mem_ln_b, w_mem_kv)
    x2 = x[0]
    x2, xb = _layer(x2, x2.astype(BF16), kv, 0, l0_w_in, None, l0_w_o, l0_ln_attn_g, l0_ln_attn_b,
                    l0_router, l0_router_bias, l0_exp_w_gu, l0_exp_w_down, l0_shared_w_gu,
                    l0_shared_w_down, l0_ln_ffn_g, l0_ln_ffn_b)
    x2, _ = _layer(x2, xb, kv, 1, l1_w_in, l1_forget_bias, l1_w_o, l1_ln_attn_g, l1_ln_attn_b,
                   l1_router, l1_router_bias, l1_exp_w_gu, l1_exp_w_down, l1_shared_w_gu,
                   l1_shared_w_down, l1_ln_ffn_g, l1_ln_ffn_b)
    return x2.reshape(batch, seq, d)
```

```python
import functools

import jax
import jax.numpy as jnp
from jax import lax
from jax.experimental import pallas as pl
from jax.experimental.pallas import tpu as pltpu

F32 = jnp.float32
BF16 = jnp.bfloat16
I32 = jnp.int32

HEAD_DIM = 128
MIX_HEADS = 12
MEM_HEADS = 4
N_EXPERTS = 64
TOP_K = 8
N_GROUPS = 8
GROUP_SIZE = N_EXPERTS // N_GROUPS
TOPK_GROUPS = 4
ROUTED_SCALE = 2.5
DEPTH = 2
ALPHA = float((2 * DEPTH) ** 0.25)
LN_EPS = 1e-5

MIB = 1024 * 1024
ATT_TILE = 256
FOX_TILE = 512
STICK_HEADS = 2
EXPERT_TILE = 256
META_RING = 4
ROW_TILE = 16
INVERT_UNROLL = 32
INVERT_STEPS = 16
EXP_ZERO = -104.0
NEG_INF = float("-inf")


def _params(semantics, vmem_mib):
    return pltpu.CompilerParams(dimension_semantics=semantics, vmem_limit_bytes=vmem_mib * MIB)


def _dot(a, b):
    return jnp.dot(a, b, preferred_element_type=F32)


def _dot_nt(a, b):
    return lax.dot_general(a, b, (((1,), (1,)), ((), ())), preferred_element_type=F32)


def _split2(x):
    hi = x.astype(BF16)
    lo = (x - hi.astype(F32)).astype(BF16)
    return hi, lo


def _split3(x):
    hi = x.astype(BF16)
    r = x - hi.astype(F32)
    mid = r.astype(BF16)
    lo = (r - mid.astype(F32)).astype(BF16)
    return hi, mid, lo


def _store_rows(ref, y):
    rows = y.shape[0]
    for c in range(ROW_TILE):
        ref[pl.ds(c, rows, stride=ROW_TILE), :] = y[:, c * HEAD_DIM:(c + 1) * HEAD_DIM]


def _load_rows(ref):
    rows = ref.shape[0] // ROW_TILE
    return jnp.concatenate([ref[pl.ds(c, rows, stride=ROW_TILE), :] for c in range(ROW_TILE)],
                           axis=1)


def _log_sigmoid(z):
    return jnp.minimum(z, 0.0) - jnp.log(1.0 + jnp.exp(-jnp.abs(z)))


def _layer_norm(y, g, b):
    mu = jnp.mean(y, axis=-1, keepdims=True)
    d = y - mu
    var = jnp.mean(d * d, axis=-1, keepdims=True)
    return d * lax.rsqrt(var + LN_EPS) * g + b


def _kv_kernel(mem_ref, g_ref, b_ref, w_ref, o_ref):
    y = _layer_norm(mem_ref[...], g_ref[...], b_ref[...])
    o_ref[...] = _dot(y.astype(BF16), w_ref[...].astype(BF16)).astype(o_ref.dtype)


def _memory_kv(mem, g, b, w):
    n, d = mem.shape
    nw = w.shape[1]
    tn = 512
    return pl.pallas_call(
        _kv_kernel,
        grid=(nw // tn,),
        in_specs=[pl.BlockSpec((n, d), lambda j: (0, 0)),
                  pl.BlockSpec((1, d), lambda j: (0, 0)),
                  pl.BlockSpec((1, d), lambda j: (0, 0)),
                  pl.BlockSpec((d, tn), lambda j: (0, j))],
        out_specs=pl.BlockSpec((n, tn), lambda j: (0, j)),
        out_shape=jax.ShapeDtypeStruct((n, nw), BF16),
        compiler_params=_params(("arbitrary",), 32),
        name="memory_kv",
    )(mem, g.reshape(1, d), b.reshape(1, d), w)


def _in_proj_kernel(x_ref, w_ref, o_ref, wbf_ref, *, k_col0, qmem_col0, scale):
    j = pl.program_id(0)
    tn = o_ref.shape[1]

    @pl.when(pl.program_id(1) == 0)
    def _():
        wbf_ref[...] = w_ref[...].astype(BF16)

    col0 = j * tn
    is_query = jnp.logical_or(col0 < k_col0, col0 >= qmem_col0)
    factor = jnp.where(is_query, scale, 1.0).astype(F32)
    o_ref[...] = (_dot(x_ref[...], wbf_ref[...]) * factor).astype(o_ref.dtype)


def _in_proj(xb, w, n_out, k_col0, qmem_col0):
    m, d = xb.shape
    tm, tn = 1024, 512
    tm = min(tm, m)
    assert k_col0 % tn == 0 and qmem_col0 % tn == 0
    kern = functools.partial(_in_proj_kernel, k_col0=k_col0, qmem_col0=qmem_col0,
                             scale=HEAD_DIM ** -0.5)
    return pl.pallas_call(
        kern,
        grid=(n_out // tn, m // tm),
        in_specs=[pl.BlockSpec((tm, d), lambda j, i: (i, 0)),
                  pl.BlockSpec((d, tn), lambda j, i: (0, j))],
        out_specs=pl.BlockSpec((tm, tn), lambda j, i: (i, j)),
        out_shape=jax.ShapeDtypeStruct((m, n_out), BF16),
        scratch_shapes=[pltpu.VMEM((d, tn), BF16)],
        compiler_params=_params(("arbitrary", "arbitrary"), 40),
        name="in_proj",
    )(xb, w)


def _gate_kernel(x_ref, w_ref, b_ref, cum_ref, cum_t_ref, carry_ref):
    tm = x_ref.shape[0]

    @pl.when(pl.program_id(0) == 0)
    def _():
        carry_ref[...] = jnp.zeros_like(carry_ref)

    xh, xl = _split2(x_ref[...])
    wh, wl = _split2(w_ref[...])
    f = _dot(xh, wh) + _dot(xh, wl) + _dot(xl, wh) + b_ref[...]
    lf = _log_sigmoid(f)
    row = lax.broadcasted_iota(I32, (tm, tm), 0)
    col = lax.broadcasted_iota(I32, (tm, tm), 1)
    lower = (col <= row).astype(BF16)
    p0, p1, p2 = _split3(lf)
    cum = _dot(lower, p0) + _dot(lower, p1) + _dot(lower, p2) + carry_ref[...]
    carry_ref[...] = cum[tm - 1:tm, :]
    cum_ref[...] = cum
    cum_t_ref[...] = cum.T[:cum_t_ref.shape[0], :]


def _forget_cumsum(x, w_f, bias):
    t, d = x.shape
    h = w_f.shape[1]
    tm = min(256, t)
    w_pad = jnp.zeros((d, HEAD_DIM), F32).at[:, :h].set(w_f)
    b_pad = jnp.zeros((1, HEAD_DIM), F32).at[0, :h].set(bias)
    return pl.pallas_call(
        _gate_kernel,
        grid=(t // tm,),
        in_specs=[pl.BlockSpec((tm, d), lambda i: (i, 0)),
                  pl.BlockSpec((d, HEAD_DIM), lambda i: (0, 0)),
                  pl.BlockSpec((1, HEAD_DIM), lambda i: (0, 0))],
        out_specs=[pl.BlockSpec((tm, HEAD_DIM), lambda i: (i, 0)),
                   pl.BlockSpec((16, tm), lambda i: (0, i))],
        out_shape=[jax.ShapeDtypeStruct((t, HEAD_DIM), F32),
                   jax.ShapeDtypeStruct((16, t), F32)],
        scratch_shapes=[pltpu.VMEM((1, HEAD_DIM), F32)],
        compiler_params=_params(("arbitrary",), 32),
        name="forget_cumsum",
    )(x, w_pad, b_pad)


def _stick_kernel(q_ref, k_ref, v_ref, o_ref):
    tq = q_ref.shape[0]
    i = pl.program_id(1)
    row = lax.broadcasted_iota(I32, (tq, tq), 0)
    col = lax.broadcasted_iota(I32, (tq, tq), 1)
    strict = col < row
    later = (row > col).astype(BF16)

    def block(j, c, acc, masked, head):
        start = pl.multiple_of(j * tq, tq)
        lanes = pl.ds(head * HEAD_DIM, HEAD_DIM)
        kb = k_ref[pl.ds(start, tq), lanes]
        vb = v_ref[pl.ds(start, tq), lanes]
        z = _dot_nt(q_ref[:, lanes], kb)
        soft = jnp.log(1.0 + jnp.exp(-jnp.abs(z)))
        log_beta = jnp.minimum(z, 0.0) - soft
        log_stay = jnp.minimum(-z, 0.0) - soft
        if masked:
            log_stay = jnp.where(strict, log_stay, 0.0)
        hi, lo = _split2(log_stay)
        log_after = _dot(hi, later) + _dot(lo, later) + c
        w = jnp.exp(log_beta + log_after)
        if masked:
            w = jnp.where(strict, w, 0.0)
        acc = acc + _dot(w.astype(BF16), vb)
        c = c + jnp.sum(log_stay, axis=1, keepdims=True)
        return c, acc

    heads = range(STICK_HEADS)
    c0 = jnp.zeros((tq, 1), F32)
    acc0 = jnp.zeros((tq, HEAD_DIM), F32)
    first = [block(i, c0, acc0, True, head) for head in heads]

    def any_alive(cs):
        worst = cs[0]
        for c in cs[1:]:
            worst = jnp.maximum(worst, c)
        return (jnp.max(worst) > EXP_ZERO).astype(I32)

    def cond(state):
        j, alive, _ = state
        return jnp.logical_and(j >= 0, alive > 0)

    def body(state):
        j, _, per_head = state
        per_head = [block(j, c, acc, False, head) for head, (c, acc) in zip(heads, per_head)]
        return j - 1, any_alive([c for c, _ in per_head]), per_head

    _, _, last = lax.while_loop(cond, body, (i - 1, any_alive([c for c, _ in first]), first))
    for head, (_, acc) in zip(heads, last):
        o_ref[:, pl.ds(head * HEAD_DIM, HEAD_DIM)] = acc.astype(o_ref.dtype)


def _stick_attention(proj, n_heads):
    t = proj.shape[0]
    tq = min(ATT_TILE, t)
    assert n_heads % STICK_HEADS == 0
    groups = n_heads // STICK_HEADS
    width = STICK_HEADS * HEAD_DIM
    return pl.pallas_call(
        _stick_kernel,
        grid=(groups, t // tq),
        in_specs=[pl.BlockSpec((tq, width), lambda h, i: (i, h)),
                  pl.BlockSpec((t, width), lambda h, i: (0, groups + h)),
                  pl.BlockSpec((t, width), lambda h, i: (0, 2 * groups + h))],
        out_specs=pl.BlockSpec((tq, width), lambda h, i: (i, h)),
        out_shape=jax.ShapeDtypeStruct((t, n_heads * HEAD_DIM), BF16),
        compiler_params=_params(("arbitrary", "arbitrary"), 40),
        name="stick_attention",
    )(proj, proj, proj)


def _fox_kernel(q_ref, k_ref, v_ref, fq_ref, fk_ref, o_ref, kmax_ref):
    tq = q_ref.shape[0]
    h = pl.program_id(0)
    i = pl.program_id(1)

    @pl.when(i == 0)
    def _():
        kk = k_ref[...].astype(F32)
        norm2 = jnp.max(jnp.sum(kk * kk, axis=1, keepdims=True), axis=0, keepdims=True)
        kmax_ref[...] = jnp.broadcast_to(jnp.sqrt(norm2), kmax_ref.shape)

    q = q_ref[...]
    lane = lax.broadcasted_iota(I32, fq_ref.shape, 1)
    fq = jnp.sum(jnp.where(lane == h, fq_ref[...], 0.0), axis=1, keepdims=True)
    qf = q.astype(F32)
    reach = jnp.sqrt(jnp.sum(qf * qf, axis=1, keepdims=True)) * kmax_ref[0:1, 0:1] + fq
    row = lax.broadcasted_iota(I32, (tq, tq), 0)
    col = lax.broadcasted_iota(I32, (tq, tq), 1)
    causal = col <= row

    def block(j, m, l, acc, masked):
        start = pl.multiple_of(j * tq, tq)
        kb = k_ref[pl.ds(start, tq), :]
        vb = v_ref[pl.ds(start, tq), :]
        fk = fk_ref[0, j]
        s = _dot_nt(q, kb) + (fq - fk)
        if masked:
            s = jnp.where(causal, s, NEG_INF)
        m_new = jnp.maximum(m, jnp.max(s, axis=1, keepdims=True))
        alpha = jnp.exp(m - m_new)
        p = jnp.exp(s - m_new)
        l = alpha * l + jnp.sum(p, axis=1, keepdims=True)
        acc = alpha * acc + _dot(p.astype(BF16), vb)
        return m_new, l, acc

    m0 = jnp.full((tq, 1), NEG_INF, F32)
    l0 = jnp.zeros((tq, 1), F32)
    acc0 = jnp.zeros((tq, HEAD_DIM), F32)
    m1, l1, acc1 = block(i, m0, l0, acc0, True)

    def alive_before(jb, m):
        last = fk_ref[0, jnp.maximum(jb - 1, 0)][:, tq - 1:tq]
        return (jnp.max(reach - last - m) > EXP_ZERO - 2.0).astype(I32)

    def cond(state):
        jb, alive, _, _, _ = state
        return jnp.logical_and(jb >= 0, alive > 0)

    def body(state):
        jb, _, m, l, acc = state
        m, l, acc = block(jb, m, l, acc, False)
        return jb - 1, alive_before(jb, m), m, l, acc

    _, _, _, l, acc = lax.while_loop(cond, body, (i - 1, alive_before(i, m1), m1, l1, acc1))
    o_ref[...] = (acc * (1.0 / l)).astype(o_ref.dtype)


def _fox_attention(proj, cum, cum_t, n_heads):
    t = proj.shape[0]
    tq = min(FOX_TILE, t)
    fk = cum_t.reshape(cum_t.shape[0], t // tq, 1, tq)
    return pl.pallas_call(
        _fox_kernel,
        grid=(n_heads, t // tq),
        in_specs=[pl.BlockSpec((tq, HEAD_DIM), lambda h, i: (i, h)),
                  pl.BlockSpec((t, HEAD_DIM), lambda h, i: (0, n_heads + h)),
                  pl.BlockSpec((t, HEAD_DIM), lambda h, i: (0, 2 * n_heads + h)),
                  pl.BlockSpec((tq, HEAD_DIM), lambda h, i: (i, 0)),
                  pl.BlockSpec((1, t // tq, 1, tq), lambda h, i: (h, 0, 0, 0))],
        out_specs=pl.BlockSpec((tq, HEAD_DIM), lambda h, i: (i, h)),
        out_shape=jax.ShapeDtypeStruct((t, n_heads * HEAD_DIM), BF16),
        scratch_shapes=[pltpu.VMEM((8, HEAD_DIM), F32)],
        compiler_params=_params(("arbitrary", "arbitrary"), 40),
        name="fox_attention",
    )(proj, proj, proj, cum, fk)


def _mem_attn_kernel(q_ref, k_ref, v_ref, o_ref):
    s = _dot_nt(q_ref[...], k_ref[...])
    m = jnp.max(s, axis=1, keepdims=True)
    p = jnp.exp(s - m)
    l = jnp.sum(p, axis=1, keepdims=True)
    o_ref[...] = (_dot(p.astype(BF16), v_ref[...]) * (1.0 / l)).astype(o_ref.dtype)


def _memory_attention(proj, kv, q_col0):
    t = proj.shape[0]
    n_mem = kv.shape[0]
    tq = min(1024, t)
    c0 = q_col0 // HEAD_DIM
    return pl.pallas_call(
        _mem_attn_kernel,
        grid=(MEM_HEADS, t // tq),
        in_specs=[pl.BlockSpec((tq, HEAD_DIM), lambda h, i: (i, c0 + h)),
                  pl.BlockSpec((n_mem, HEAD_DIM), lambda h, i: (0, h)),
                  pl.BlockSpec((n_mem, HEAD_DIM), lambda h, i: (0, MEM_HEADS + h))],
        out_specs=pl.BlockSpec((tq, HEAD_DIM), lambda h, i: (i, h)),
        out_shape=jax.ShapeDtypeStruct((t, MEM_HEADS * HEAD_DIM), BF16),
        compiler_params=_params(("arbitrary", "arbitrary"), 32),
        name="memory_attention",
    )(proj, kv, kv)


def _out_proj_kernel(om_ref, oc_ref, w_ref, x_ref, g_ref, b_ref, y_ref, yp_ref):
    n_mix = om_ref.shape[1]
    mix = _dot(om_ref[...], w_ref[:n_mix, :]) + _dot(oc_ref[...], w_ref[n_mix:, :])
    y = _layer_norm(ALPHA * x_ref[...] + mix, g_ref[...], b_ref[...])
    y_ref[...] = y
    _store_rows(yp_ref, y)


def _out_proj_norm(o_mix, o_mem, w_o_bf, x, g, b):
    t, d = x.shape
    tm = min(512, t)
    n_mix, n_mem = o_mix.shape[1], o_mem.shape[1]
    assert d == ROW_TILE * HEAD_DIM
    return pl.pallas_call(
        _out_proj_kernel,
        grid=(t // tm,),
        in_specs=[pl.BlockSpec((tm, n_mix), lambda i: (i, 0)),
                  pl.BlockSpec((tm, n_mem), lambda i: (i, 0)),
                  pl.BlockSpec((n_mix + n_mem, d), lambda i: (0, 0)),
                  pl.BlockSpec((tm, d), lambda i: (i, 0)),
                  pl.BlockSpec((1, d), lambda i: (0, 0)),
                  pl.BlockSpec((1, d), lambda i: (0, 0))],
        out_specs=[pl.BlockSpec((tm, d), lambda i: (i, 0)),
                   pl.BlockSpec((tm * ROW_TILE, HEAD_DIM), lambda i: (i, 0))],
        out_shape=[jax.ShapeDtypeStruct((t, d), F32),
                   jax.ShapeDtypeStruct((t * ROW_TILE, HEAD_DIM), F32)],
        compiler_params=_params(("arbitrary",), 48),
        name="out_proj_norm",
    )(o_mix, o_mem, w_o_bf, x, g.reshape(1, d), b.reshape(1, d))


def _router_kernel(x_ref, wh_ref, wl_ref, b_ref, eid_ref, rank_ref, gate_ref, cnt_ref, carry_ref):
    tm = x_ref.shape[0]

    @pl.when(pl.program_id(0) == 0)
    def _():
        carry_ref[...] = jnp.zeros_like(carry_ref)

    xh, xl = _split2(x_ref[...])
    wh, wl = wh_ref[...], wl_ref[...]
    logits = _dot_nt(wh, xh) + _dot_nt(wh, xl) + _dot_nt(wl, xh)
    scores = 1.0 / (1.0 + jnp.exp(-logits))
    sel = scores + b_ref[:, 0:1]

    group_score = []
    for g in range(N_GROUPS):
        v = sel[g * GROUP_SIZE:(g + 1) * GROUP_SIZE, :]
        m1 = jnp.max(v, axis=0, keepdims=True)
        is_max = v == m1
        n_max = jnp.sum(is_max.astype(F32), axis=0, keepdims=True)
        m2 = jnp.max(jnp.where(is_max, NEG_INF, v), axis=0, keepdims=True)
        group_score.append(m1 + jnp.where(n_max >= 2.0, m1, m2))
    masked = []
    for g in range(N_GROUPS):
        ahead = jnp.zeros((1, tm), F32)
        for g2 in range(N_GROUPS):
            if g2 == g:
                continue
            beats = group_score[g2] > group_score[g]
            if g2 < g:
                beats = jnp.logical_or(beats, group_score[g2] == group_score[g])
            ahead = ahead + beats.astype(F32)
        v = sel[g * GROUP_SIZE:(g + 1) * GROUP_SIZE, :]
        masked.append(jnp.where(ahead < float(TOPK_GROUPS), v, NEG_INF))
    msel = jnp.concatenate(masked, axis=0)

    e_idx = lax.broadcasted_iota(I32, (N_EXPERTS, tm), 0)
    ahead = jnp.zeros((N_EXPERTS, tm), F32)
    for e2 in range(N_EXPERTS):
        other = msel[e2:e2 + 1, :]
        beats = jnp.logical_or(other > msel, jnp.logical_and(other == msel, e_idx > e2))
        ahead = ahead + beats.astype(F32)
    chosen = ahead < float(TOP_K)
    chosen_f = chosen.astype(F32)

    w = jnp.where(chosen, scores, 0.0)
    gates = w / jnp.sum(w, axis=0, keepdims=True) * ROUTED_SCALE

    chosen_b = chosen_f.astype(BF16)
    r64 = lax.broadcasted_iota(I32, (N_EXPERTS, N_EXPERTS), 0)
    c64 = lax.broadcasted_iota(I32, (N_EXPERTS, N_EXPERTS), 1)
    choice = _dot((c64 < r64).astype(BF16), chosen_b)
    rt = lax.broadcasted_iota(I32, (tm, tm), 0)
    ct = lax.broadcasted_iota(I32, (tm, tm), 1)
    rank = _dot(chosen_b, (rt < ct).astype(BF16)) + carry_ref[:, 0:1]
    carry_ref[...] = carry_ref[...] + jnp.sum(chosen_f, axis=1, keepdims=True)
    cnt_ref[...] = carry_ref[...].astype(I32)

    e_f = e_idx.astype(F32)
    eids, ranks, gts = [], [], []
    for k in range(TOP_K):
        pick = jnp.logical_and(chosen, choice == float(k))
        eids.append(jnp.sum(jnp.where(pick, e_f, 0.0), axis=0, keepdims=True))
        ranks.append(jnp.sum(jnp.where(pick, rank, 0.0), axis=0, keepdims=True))
        gts.append(jnp.sum(jnp.where(pick, gates, 0.0), axis=0, keepdims=True))
    eid_ref[...] = jnp.concatenate(eids, axis=0).astype(I32)
    rank_ref[...] = jnp.concatenate(ranks, axis=0).astype(I32)
    gate_ref[...] = jnp.concatenate(gts, axis=0)


def _router(x1, w_router, router_bias):
    t, d = x1.shape
    tm = min(512, t)
    wt = w_router.T
    wh, wl = _split2(wt)
    bias = jnp.broadcast_to(router_bias.astype(F32)[:, None], (N_EXPERTS, HEAD_DIM))
    return pl.pallas_call(
        _router_kernel,
        grid=(t // tm,),
        in_specs=[pl.BlockSpec((tm, d), lambda i: (i, 0)),
                  pl.BlockSpec((N_EXPERTS, d), lambda i: (0, 0)),
                  pl.BlockSpec((N_EXPERTS, d), lambda i: (0, 0)),
                  pl.BlockSpec((N_EXPERTS, HEAD_DIM), lambda i: (0, 0))],
        out_specs=[pl.BlockSpec((TOP_K, tm), lambda i: (0, i)),
                   pl.BlockSpec((TOP_K, tm), lambda i: (0, i)),
                   pl.BlockSpec((TOP_K, tm), lambda i: (0, i)),
                   pl.BlockSpec((N_EXPERTS, HEAD_DIM), lambda i: (0, 0))],
        out_shape=[jax.ShapeDtypeStruct((TOP_K, t), I32),
                   jax.ShapeDtypeStruct((TOP_K, t), I32),
                   jax.ShapeDtypeStruct((TOP_K, t), F32),
                   jax.ShapeDtypeStruct((N_EXPERTS, HEAD_DIM), I32)],
        scratch_shapes=[pltpu.VMEM((N_EXPERTS, HEAD_DIM), F32)],
        compiler_params=_params(("arbitrary",), 40),
        name="router",
    )(x1, wh, wl, bias)


def _expert_kernel(te_ref, nu_ref, meta_hbm, x_hbm, wgu_ref, wd_ref, y_hbm,
                   meta_s, xa, xb, ya, yb, wgu_bf, wd_bf, msem, gsem, ssem, *, n_tok):
    tm = xa.shape[0] // ROW_TILE
    ff = wd_ref.shape[0]
    j = pl.program_id(0)
    nu = nu_ref[0]

    def meta_copy(a):
        slot = a % META_RING
        return pltpu.make_async_copy(meta_hbm.at[pl.ds(a * 2 * tm, 2 * tm)],
                                     meta_s.at[pl.ds(slot * 2 * tm, 2 * tm)], msem.at[slot])

    def start_gathers(a, x_dst, sem):
        base = (a % META_RING) * 2 * tm
        for r in range(tm):
            src = pl.multiple_of(meta_s[base + r], ROW_TILE)
            pltpu.async_copy(x_hbm.at[pl.ds(src, ROW_TILE), :],
                             x_dst.at[pl.ds(r * ROW_TILE, ROW_TILE), :], sem, priority=r % 2)

    def start_scatters(a, y_src, sem):
        base = (a % META_RING) * 2 * tm + tm
        for r in range(tm):
            dest = pl.multiple_of(meta_s[base + r], ROW_TILE)
            pltpu.async_copy(y_src.at[pl.ds(r * ROW_TILE, ROW_TILE), :],
                             y_hbm.at[pl.ds(dest, ROW_TILE), :], sem, priority=r % 2)

    def wait_gathers(x_dst, sem):
        pltpu.make_async_copy(x_hbm.at[pl.ds(0, tm * ROW_TILE), :], x_dst, sem).wait()

    def wait_scatters(y_src, sem):
        pltpu.make_async_copy(y_src, y_hbm.at[pl.ds(0, tm * ROW_TILE), :], sem).wait()

    @pl.when(j == 0)
    def _():
        ya[...] = jnp.zeros_like(ya)
        yb[...] = jnp.zeros_like(yb)
        pltpu.make_async_copy(ya, y_hbm.at[pl.ds(TOP_K * n_tok * ROW_TILE, tm * ROW_TILE), :],
                              ssem.at[0]).start()
        meta_copy(0).start()
        meta_copy(1).start()
        meta_copy(2).start()
        meta_copy(0).wait()
        meta_copy(1).wait()
        start_gathers(1, xa, gsem.at[0])

    def step(x_cur, y_cur, x_nxt, y_prv, p):
        wait_gathers(x_cur, gsem.at[p])
        wait_scatters(y_cur, ssem.at[p])
        start_gathers(j + 2, x_nxt, gsem.at[1 - p])
        start_scatters(j, y_prv, ssem.at[1 - p])
        gu = _dot(_load_rows(x_cur).astype(BF16), wgu_bf[...])
        a = gu[:, :ff]
        u = gu[:, ff:]
        hidden = (a * (1.0 / (1.0 + jnp.exp(-a))) * u).astype(BF16)
        _store_rows(y_cur, _dot(hidden, wd_bf[...]))

        @pl.when(j == nu - 1)
        def _():
            start_scatters(j + 1, y_cur, ssem.at[p])
            wait_scatters(y_prv, ssem.at[1 - p])
            wait_scatters(y_cur, ssem.at[p])
            wait_gathers(x_nxt, gsem.at[1 - p])
            meta_copy(j + 3).wait()

    @pl.when(j < nu)
    def _():
        meta_copy(j + 3).start()
        meta_copy(j + 2).wait()
        new_expert = jnp.logical_or(j == 0, te_ref[j] != te_ref[jnp.maximum(j - 1, 0)])

        @pl.when(new_expert)
        def _():
            wgu_bf[...] = wgu_ref[...].astype(BF16)
            wd_bf[...] = wd_ref[...].astype(BF16)

        @pl.when(j % 2 == 0)
        def _():
            step(xa, ya, xb, yb, 0)

        @pl.when(j % 2 == 1)
        def _():
            step(xb, yb, xa, ya, 1)


def _routed_experts(x_rows, meta, tile_expert, n_used, w_gu, w_down):
    t = x_rows.shape[0] // ROW_TILE
    d = w_gu.shape[1]
    ff = w_down.shape[1]
    tm = EXPERT_TILE
    n_tiles = tile_expert.shape[0]
    kern = functools.partial(_expert_kernel, n_tok=t)
    row_buf = pltpu.VMEM((tm * ROW_TILE, HEAD_DIM), F32)
    grid_spec = pltpu.PrefetchScalarGridSpec(
        num_scalar_prefetch=2,
        grid=(n_tiles,),
        in_specs=[pl.BlockSpec(memory_space=pl.ANY),
                  pl.BlockSpec(memory_space=pl.ANY),
                  pl.BlockSpec((None, d, 2 * ff), lambda j, te, nu: (te[j], 0, 0)),
                  pl.BlockSpec((None, ff, d), lambda j, te, nu: (te[j], 0, 0))],
        out_specs=pl.BlockSpec(memory_space=pl.ANY),
        scratch_shapes=[pltpu.SMEM((META_RING * 2 * tm,), I32),
                        row_buf, row_buf, row_buf, row_buf,
                        pltpu.VMEM((d, 2 * ff), BF16),
                        pltpu.VMEM((ff, d), BF16),
                        pltpu.SemaphoreType.DMA((META_RING,)),
                        pltpu.SemaphoreType.DMA((2,)),
                        pltpu.SemaphoreType.DMA((2,))],
    )
    return pl.pallas_call(
        kern,
        grid_spec=grid_spec,
        out_shape=jax.ShapeDtypeStruct(((TOP_K * t + 2 * tm) * ROW_TILE, HEAD_DIM), F32),
        compiler_params=_params(("arbitrary",), 56),
        name="routed_experts",
    )(tile_expert, n_used, meta, x_rows, w_gu, w_down)


def _invert_kernel(slot_ref, out_ref):
    unroll = INVERT_UNROLL
    step = pl.program_id(0)
    n_clear = out_ref.shape[0] // INVERT_STEPS
    n_place = slot_ref.shape[0] // INVERT_STEPS

    @pl.when(step < INVERT_STEPS)
    def _():
        base = step * n_clear

        def clear(i, _):
            for u in range(unroll):
                out_ref[base + i * unroll + u] = 0
            return 0
        lax.fori_loop(0, n_clear // unroll, clear, 0)

    @pl.when(step >= INVERT_STEPS)
    def _():
        base = (step - INVERT_STEPS) * n_place

        def place(i, _):
            for u in range(unroll):
                src = base + i * unroll + u
                out_ref[slot_ref[src]] = src
            return 0
        lax.fori_loop(0, n_place // unroll, place, 0)


def _invert_slots(slot, n_slots):
    chunk = INVERT_STEPS * INVERT_UNROLL
    assert slot.shape[0] % chunk == 0 and n_slots % chunk == 0
    return pl.pallas_call(
        _invert_kernel,
        grid=(2 * INVERT_STEPS,),
        in_specs=[pl.BlockSpec(memory_space=pltpu.SMEM)],
        out_specs=pl.BlockSpec(memory_space=pltpu.SMEM),
        out_shape=jax.ShapeDtypeStruct((n_slots,), I32),
        compiler_params=pltpu.CompilerParams(dimension_semantics=("arbitrary",)),
        name="invert_slots",
    )(slot)


def _dispatch_plan(eid, rank, counts, t):
    tm = EXPERT_TILE
    n_tiles = (TOP_K * t) // tm + N_EXPERTS
    n_slots = n_tiles * tm
    assert t & (t - 1) == 0
    padded = ((counts + tm - 1) // tm) * tm
    ends = jnp.cumsum(padded)
    offs = ends - padded
    n_used = (ends[-1] // tm).astype(I32)
    tiles = jnp.arange(n_tiles, dtype=I32)
    experts = jnp.arange(N_EXPERTS, dtype=I32)
    tile_start = jnp.minimum(tiles, n_used - 1) * tm
    tile_expert = jnp.sum((ends[None, :] <= tile_start[:, None]).astype(I32), axis=1)
    tile_expert = jnp.clip(tile_expert, 0, N_EXPERTS - 1)
    of_tile = tile_expert[:, None] == experts[None, :]
    tile_count = jnp.sum(jnp.where(of_tile, counts[None, :], 0), axis=1)
    tile_offs = jnp.sum(jnp.where(of_tile, offs[None, :], 0), axis=1)
    tile_valid = jnp.clip(tile_count - (tiles * tm - tile_offs), 0, tm).astype(I32)
    slot = rank + jnp.sum(jnp.where(eid[:, :, None] == experts, offs, 0), axis=2)
    packed = _invert_slots(slot.reshape(-1), n_slots).reshape(n_tiles, tm)
    row = lax.broadcasted_iota(I32, (n_tiles, tm), 1)
    gather_rows = packed & (t - 1)
    spare = TOP_K * t + (tiles[:, None] % 2) * tm + row
    scatter_rows = jnp.where(row < tile_valid[:, None], packed, spare)
    plan = jnp.stack([gather_rows, scatter_rows], axis=1)
    dummy = jnp.stack([jnp.zeros((1, tm), I32), TOP_K * t + tm + row[:1]], axis=1)
    tail = jnp.concatenate([dummy, dummy], axis=0)
    meta = jnp.concatenate([dummy, plan, tail], axis=0).reshape(-1) * ROW_TILE
    return meta, tile_expert, n_used.reshape(1)


def _combine_kernel(x_ref, gate_ref, wgu_ref, wd_ref, g_ref, b_ref, *refs):
    y_refs = refs[:TOP_K]
    o_ref, ob_ref, stage_ref = refs[TOP_K:]
    tm = x_ref.shape[0]
    ff = wd_ref.shape[0]
    x = x_ref[...]
    gu = _dot(x.astype(BF16), wgu_ref[...])
    a = gu[:, :ff]
    u = gu[:, ff:]
    hidden = (a * (1.0 / (1.0 + jnp.exp(-a))) * u).astype(BF16)
    ffn = _dot(hidden, wd_ref[...])
    rows = tm * ROW_TILE
    gates = gate_ref[...]
    routed = jnp.zeros((rows, HEAD_DIM), F32)
    for k in range(TOP_K):
        gate_lanes = jnp.broadcast_to(gates[:, k:k + 1], (tm, HEAD_DIM))
        gate_rows = jnp.broadcast_to(gate_lanes[:, None, :], (tm, ROW_TILE, HEAD_DIM))
        routed = routed + gate_rows.reshape(rows, HEAD_DIM) * y_refs[k][...]
    stage_ref[...] = routed
    out = _layer_norm(ALPHA * x + (ffn + _load_rows(stage_ref)), g_ref[...], b_ref[...])
    o_ref[...] = out
    ob_ref[...] = out.astype(BF16)


def _combine_norm(x1, gates_tk, y, w_gu_bf, w_down_bf, g, b):
    t, d = x1.shape
    ff = w_down_bf.shape[0]
    tm = min(128, t)
    nb = t // tm
    y_specs = [pl.BlockSpec((tm * ROW_TILE, HEAD_DIM), functools.partial(lambda i, k: (k * nb + i, 0), k=k))
               for k in range(TOP_K)]
    return pl.pallas_call(
        _combine_kernel,
        grid=(nb,),
        in_specs=[pl.BlockSpec((tm, d), lambda i: (i, 0)),
                  pl.BlockSpec((tm, TOP_K), lambda i: (i, 0)),
                  pl.BlockSpec((d, 2 * ff), lambda i: (0, 0)),
                  pl.BlockSpec((ff, d), lambda i: (0, 0)),
                  pl.BlockSpec((1, d), lambda i: (0, 0)),
                  pl.BlockSpec((1, d), lambda i: (0, 0))] + y_specs,
        out_specs=[pl.BlockSpec((tm, d), lambda i: (i, 0)),
                   pl.BlockSpec((tm, d), lambda i: (i, 0))],
        out_shape=[jax.ShapeDtypeStruct((t, d), F32),
                   jax.ShapeDtypeStruct((t, d), BF16)],
        scratch_shapes=[pltpu.VMEM((tm * ROW_TILE, HEAD_DIM), F32)],
        compiler_params=_params(("arbitrary",), 48),
        name="combine_norm",
    )(x1, gates_tk, w_gu_bf, w_down_bf, g.reshape(1, d), b.reshape(1, d), *([y] * TOP_K))


def _moe_block(x1, x1_rows, router, router_bias, exp_w_gu, exp_w_down, shared_w_gu, shared_w_down,
               g, b):
    t = x1.shape[0]
    eid, rank, gate, counts = _router(x1, router, router_bias)
    meta, tile_expert, n_used = _dispatch_plan(eid, rank, counts[:, 0], t)
    y = _routed_experts(x1_rows, meta, tile_expert, n_used, exp_w_gu, exp_w_down)
    return _combine_norm(x1, gate.T, y, shared_w_gu.astype(BF16), shared_w_down.astype(BF16), g, b)


def _layer(x, xb, kv, mixer, w_in, forget_bias, w_o, ln_attn_g, ln_attn_b, router, router_bias,
           exp_w_gu, exp_w_down, shared_w_gu, shared_w_down, ln_ffn_g, ln_ffn_b):
    mix_w = MIX_HEADS * HEAD_DIM
    n_proj = 3 * mix_w + MEM_HEADS * HEAD_DIM
    proj = _in_proj(xb, w_in, n_proj, mix_w, 3 * mix_w)
    if mixer == 0:
        o_mix = _stick_attention(proj, MIX_HEADS)
    else:
        cum, cum_t = _forget_cumsum(x, w_in[:, n_proj:], forget_bias)
        o_mix = _fox_attention(proj, cum, cum_t, MIX_HEADS)
    o_mem = _memory_attention(proj, kv, 3 * mix_w)
    x1, x1_rows = _out_proj_norm(o_mix, o_mem, w_o.astype(BF16), x, ln_attn_g, ln_attn_b)
    return _moe_block(x1, x1_rows, router, router_bias, exp_w_gu, exp_w_down, shared_w_gu,
                      shared_w_down, ln_ffn_g, ln_ffn_b)


def kernel(x, mem, mem_ln_g, mem_ln_b, w_mem_kv,
           l0_w_in, l0_w_o, l0_ln_attn_g, l0_ln_attn_b, l0_router, l0_router_bias,
           l0_exp_w_gu, l0_exp_w_down, l0_shared_w_gu, l0_shared_w_down, l0_ln_ffn_g, l0_ln_ffn_b,
           l1_w_in, l1_forget_bias, l1_w_o, l1_ln_attn_g, l1_ln_attn_b, l1_router, l1_router_bias,
           l1_exp_w_gu, l1_exp_w_down, l1_shared_w_gu, l1_shared_w_down, l1_ln_ffn_g, l1_ln_ffn_b):
    batch, seq, d = x.shape
    assert batch == 1
    kv = _memory_kv(mem[0], mem_ln_g, mem_ln_b, w_mem_kv)
    x2 = x[0]
    x2, xb = _layer(x2, x2.astype(BF16), kv, 0, l0_w_in, None, l0_w_o, l0_ln_attn_g, l0_ln_attn_b,
                    l0_router, l0_router_bias, l0_exp_w_gu, l0_exp_w_down, l0_shared_w_gu,
                    l0_shared_w_down, l0_ln_ffn_g, l0_ln_ffn_b)
    x2, _ = _layer(x2, xb, kv, 1, l1_w_in, l1_forget_bias, l1_w_o, l1_ln_attn_g, l1_ln_attn_b,
                   l1_router, l1_router_bias, l1_exp_w_gu, l1_exp_w_down, l1_shared_w_gu,
                   l1_shared_w_down, l1_ln_ffn_g, l1_ln_ffn_b)
    return x2.reshape(batch, seq, d)
```

```python
import functools

import jax
import jax.numpy as jnp
from jax import lax
from jax.experimental import pallas as pl
from jax.experimental.pallas import tpu as pltpu

F32 = jnp.float32
BF16 = jnp.bfloat16
I32 = jnp.int32

HEAD_DIM = 128
MIX_HEADS = 12
MEM_HEADS = 4
N_EXPERTS = 64
TOP_K = 8
N_GROUPS = 8
GROUP_SIZE = N_EXPERTS // N_GROUPS
TOPK_GROUPS = 4
ROUTED_SCALE = 2.5
DEPTH = 2
ALPHA = float((2 * DEPTH) ** 0.25)
LN_EPS = 1e-5

MIB = 1024 * 1024
ATT_TILE = 256
FOX_TILE = 512
STICK_HEADS = 2
EXPERT_TILE = 256
META_RING = 4
ROW_TILE = 16
INVERT_UNROLL = 32
INVERT_STEPS = 16
EXP_ZERO = -104.0
NEG_INF = float("-inf")


def _params(semantics, vmem_mib):
    return pltpu.CompilerParams(dimension_semantics=semantics, vmem_limit_bytes=vmem_mib * MIB)


def _dot(a, b):
    return jnp.dot(a, b, preferred_element_type=F32)


def _dot_nt(a, b):
    return lax.dot_general(a, b, (((1,), (1,)), ((), ())), preferred_element_type=F32)


def _split2(x):
    hi = x.astype(BF16)
    lo = (x - hi.astype(F32)).astype(BF16)
    return hi, lo


def _split3(x):
    hi = x.astype(BF16)
    r = x - hi.astype(F32)
    mid = r.astype(BF16)
    lo = (r - mid.astype(F32)).astype(BF16)
    return hi, mid, lo


def _store_rows(ref, y):
    rows = y.shape[0]
    for c in range(ROW_TILE):
        ref[pl.ds(c, rows, stride=ROW_TILE), :] = y[:, c * HEAD_DIM:(c + 1) * HEAD_DIM]


def _load_rows(ref):
    rows = ref.shape[0] // ROW_TILE
    return jnp.concatenate([ref[pl.ds(c, rows, stride=ROW_TILE), :] for c in range(ROW_TILE)],
                           axis=1)


def _log_sigmoid(z):
    return jnp.minimum(z, 0.0) - jnp.log(1.0 + jnp.exp(-jnp.abs(z)))


def _layer_norm(y, g, b):
    mu = jnp.mean(y, axis=-1, keepdims=True)
    d = y - mu
    var = jnp.mean(d * d, axis=-1, keepdims=True)
    return d * lax.rsqrt(var + LN_EPS) * g + b


def _kv_kernel(mem_ref, g_ref, b_ref, w_ref, o_ref):
    y = _layer_norm(mem_ref[...], g_ref[...], b_ref[...])
    o_ref[...] = _dot(y.astype(BF16), w_ref[...].astype(BF16)).astype(o_ref.dtype)


def _memory_kv(mem, g, b, w):
    n, d = mem.shape
    nw = w.shape[1]
    tn = 512
    return pl.pallas_call(
        _kv_kernel,
        grid=(nw // tn,),
        in_specs=[pl.BlockSpec((n, d), lambda j: (0, 0)),
                  pl.BlockSpec((1, d), lambda j: (0, 0)),
                  pl.BlockSpec((1, d), lambda j: (0, 0)),
                  pl.BlockSpec((d, tn), lambda j: (0, j))],
        out_specs=pl.BlockSpec((n, tn), lambda j: (0, j)),
        out_shape=jax.ShapeDtypeStruct((n, nw), BF16),
        compiler_params=_params(("arbitrary",), 32),
        name="memory_kv",
    )(mem, g.reshape(1, d), b.reshape(1, d), w)


def _in_proj_kernel(x_ref, w_ref, o_ref, wbf_ref, *, k_col0, qmem_col0, scale):
    j = pl.program_id(0)
    tn = o_ref.shape[1]

    @pl.when(pl.program_id(1) == 0)
    def _():
        wbf_ref[...] = w_ref[...].astype(BF16)

    col0 = j * tn
    is_query = jnp.logical_or(col0 < k_col0, col0 >= qmem_col0)
    factor = jnp.where(is_query, scale, 1.0).astype(F32)
    o_ref[...] = (_dot(x_ref[...], wbf_ref[...]) * factor).astype(o_ref.dtype)


def _in_proj(xb, w, n_out, k_col0, qmem_col0):
    m, d = xb.shape
    tm, tn = 1024, 512
    tm = min(tm, m)
    assert k_col0 % tn == 0 and qmem_col0 % tn == 0
    kern = functools.partial(_in_proj_kernel, k_col0=k_col0, qmem_col0=qmem_col0,
                             scale=HEAD_DIM ** -0.5)
    return pl.pallas_call(
        kern,
        grid=(n_out // tn, m // tm),
        in_specs=[pl.BlockSpec((tm, d), lambda j, i: (i, 0)),
                  pl.BlockSpec((d, tn), lambda j, i: (0, j))],
        out_specs=pl.BlockSpec((tm, tn), lambda j, i: (i, j)),
        out_shape=jax.ShapeDtypeStruct((m, n_out), BF16),
        scratch_shapes=[pltpu.VMEM((d, tn), BF16)],
        compiler_params=_params(("arbitrary", "arbitrary"), 40),
        name="in_proj",
    )(xb, w)


def _gate_kernel(x_ref, w_ref, b_ref, cum_ref, cum_t_ref, carry_ref):
    tm = x_ref.shape[0]

    @pl.when(pl.program_id(0) == 0)
    def _():
        carry_ref[...] = jnp.zeros_like(carry_ref)

    xh, xl = _split2(x_ref[...])
    wh, wl = _split2(w_ref[...])
    f = _dot(xh, wh) + _dot(xh, wl) + _dot(xl, wh) + b_ref[...]
    lf = _log_sigmoid(f)
    row = lax.broadcasted_iota(I32, (tm, tm), 0)
    col = lax.broadcasted_iota(I32, (tm, tm), 1)
    lower = (col <= row).astype(BF16)
    p0, p1, p2 = _split3(lf)
    cum = _dot(lower, p0) + _dot(lower, p1) + _dot(lower, p2) + carry_ref[...]
    carry_ref[...] = cum[tm - 1:tm, :]
    cum_ref[...] = cum
    cum_t_ref[...] = cum.T[:cum_t_ref.shape[0], :]


def _forget_cumsum(x, w_f, bias):
    t, d = x.shape
    h = w_f.shape[1]
    tm = min(256, t)
    w_pad = jnp.zeros((d, HEAD_DIM), F32).at[:, :h].set(w_f)
    b_pad = jnp.zeros((1, HEAD_DIM), F32).at[0, :h].set(bias)
    return pl.pallas_call(
        _gate_kernel,
        grid=(t // tm,),
        in_specs=[pl.BlockSpec((tm, d), lambda i: (i, 0)),
                  pl.BlockSpec((d, HEAD_DIM), lambda i: (0, 0)),
                  pl.BlockSpec((1, HEAD_DIM), lambda i: (0, 0))],
        out_specs=[pl.BlockSpec((tm, HEAD_DIM), lambda i: (i, 0)),
                   pl.BlockSpec((16, tm), lambda i: (0, i))],
        out_shape=[jax.ShapeDtypeStruct((t, HEAD_DIM), F32),
                   jax.ShapeDtypeStruct((16, t), F32)],
        scratch_shapes=[pltpu.VMEM((1, HEAD_DIM), F32)],
        compiler_params=_params(("arbitrary",), 32),
        name="forget_cumsum",
    )(x, w_pad, b_pad)


def _stick_kernel(q_ref, k_ref, v_ref, o_ref):
    tq = q_ref.shape[0]
    i = pl.program_id(1)
    row = lax.broadcasted_iota(I32, (tq, tq), 0)
    col = lax.broadcasted_iota(I32, (tq, tq), 1)
    strict = col < row
    later = (row > col).astype(BF16)

    def block(j, c, acc, masked, head):
        start = pl.multiple_of(j * tq, tq)
        lanes = pl.ds(head * HEAD_DIM, HEAD_DIM)
        kb = k_ref[pl.ds(start, tq), lanes]
        vb = v_ref[pl.ds(start, tq), lanes]
        z = _dot_nt(q_ref[:, lanes], kb)
        soft = jnp.log(1.0 + jnp.exp(-jnp.abs(z)))
        log_beta = jnp.minimum(z, 0.0) - soft
        log_stay = jnp.minimum(-z, 0.0) - soft
        if masked:
            log_stay = jnp.where(strict, log_stay, 0.0)
        hi, lo = _split2(log_stay)
        log_after = _dot(hi, later) + _dot(lo, later) + c
        w = jnp.exp(log_beta + log_after)
        if masked:
            w = jnp.where(strict, w, 0.0)
        acc = acc + _dot(w.astype(BF16), vb)
        c = c + jnp.sum(log_stay, axis=1, keepdims=True)
        return c, acc

    heads = range(STICK_HEADS)
    c0 = jnp.zeros((tq, 1), F32)
    acc0 = jnp.zeros((tq, HEAD_DIM), F32)
    first = [block(i, c0, acc0, True, head) for head in heads]

    def any_alive(cs):
        worst = cs[0]
        for c in cs[1:]:
            worst = jnp.maximum(worst, c)
        return (jnp.max(worst) > EXP_ZERO).astype(I32)

    def cond(state):
        j, alive, _ = state
        return jnp.logical_and(j >= 0, alive > 0)

    def body(state):
        j, _, per_head = state
        per_head = [block(j, c, acc, False, head) for head, (c, acc) in zip(heads, per_head)]
        return j - 1, any_alive([c for c, _ in per_head]), per_head

    _, _, last = lax.while_loop(cond, body, (i - 1, any_alive([c for c, _ in first]), first))
    for head, (_, acc) in zip(heads, last):
        o_ref[:, pl.ds(head * HEAD_DIM, HEAD_DIM)] = acc.astype(o_ref.dtype)


def _stick_attention(proj, n_heads):
    t = proj.shape[0]
    tq = min(ATT_TILE, t)
    assert n_heads % STICK_HEADS == 0
    groups = n_heads // STICK_HEADS
    width = STICK_HEADS * HEAD_DIM
    return pl.pallas_call(
        _stick_kernel,
        grid=(groups, t // tq),
        in_specs=[pl.BlockSpec((tq, width), lambda h, i: (i, h)),
                  pl.BlockSpec((t, width), lambda h, i: (0, groups + h)),
                  pl.BlockSpec((t, width), lambda h, i: (0, 2 * groups + h))],
        out_specs=pl.BlockSpec((tq, width), lambda h, i: (i, h)),
        out_shape=jax.ShapeDtypeStruct((t, n_heads * HEAD_DIM), BF16),
        compiler_params=_params(("arbitrary", "arbitrary"), 40),
        name="stick_attention",
    )(proj, proj, proj)


def _fox_kernel(q_ref, k_ref, v_ref, fq_ref, fk_ref, o_ref, kmax_ref):
    tq = q_ref.shape[0]
    h = pl.program_id(0)
    i = pl.program_id(1)

    @pl.when(i == 0)
    def _():
        kk = k_ref[...].astype(F32)
        norm2 = jnp.max(jnp.sum(kk * kk, axis=1, keepdims=True), axis=0, keepdims=True)
        kmax_ref[...] = jnp.broadcast_to(jnp.sqrt(norm2), kmax_ref.shape)

    q = q_ref[...]
    lane = lax.broadcasted_iota(I32, fq_ref.shape, 1)
    fq = jnp.sum(jnp.where(lane == h, fq_ref[...], 0.0), axis=1, keepdims=True)
    qf = q.astype(F32)
    reach = jnp.sqrt(jnp.sum(qf * qf, axis=1, keepdims=True)) * kmax_ref[0:1, 0:1] + fq
    row = lax.broadcasted_iota(I32, (tq, tq), 0)
    col = lax.broadcasted_iota(I32, (tq, tq), 1)
    causal = col <= row

    def block(j, m, l, acc, masked):
        start = pl.multiple_of(j * tq, tq)
        kb = k_ref[pl.ds(start, tq), :]
        vb = v_ref[pl.ds(start, tq), :]
        fk = fk_ref[0, j]
        s = _dot_nt(q, kb) + (fq - fk)
        if masked:
            s = jnp.where(causal, s, NEG_INF)
        m_new = jnp.maximum(m, jnp.max(s, axis=1, keepdims=True))
        alpha = jnp.exp(m - m_new)
        p = jnp.exp(s - m_new)
        l = alpha * l + jnp.sum(p, axis=1, keepdims=True)
        acc = alpha * acc + _dot(p.astype(BF16), vb)
        return m_new, l, acc

    m0 = jnp.full((tq, 1), NEG_INF, F32)
    l0 = jnp.zeros((tq, 1), F32)
    acc0 = jnp.zeros((tq, HEAD_DIM), F32)
    m1, l1, acc1 = block(i, m0, l0, acc0, True)

    def alive_before(jb, m):
        last = fk_ref[0, jnp.maximum(jb - 1, 0)][:, tq - 1:tq]
        return (jnp.max(reach - last - m) > EXP_ZERO - 2.0).astype(I32)

    def cond(state):
        jb, alive, _, _, _ = state
        return jnp.logical_and(jb >= 0, alive > 0)

    def body(state):
        jb, _, m, l, acc = state
        m, l, acc = block(jb, m, l, acc, False)
        return jb - 1, alive_before(jb, m), m, l, acc

    _, _, _, l, acc = lax.while_loop(cond, body, (i - 1, alive_before(i, m1), m1, l1, acc1))
    o_ref[...] = (acc * (1.0 / l)).astype(o_ref.dtype)


def _fox_attention(proj, cum, cum_t, n_heads):
    t = proj.shape[0]
    tq = min(FOX_TILE, t)
    fk = cum_t.reshape(cum_t.shape[0], t // tq, 1, tq)
    return pl.pallas_call(
        _fox_kernel,
        grid=(n_heads, t // tq),
        in_specs=[pl.BlockSpec((tq, HEAD_DIM), lambda h, i: (i, h)),
                  pl.BlockSpec((t, HEAD_DIM), lambda h, i: (0, n_heads + h)),
                  pl.BlockSpec((t, HEAD_DIM), lambda h, i: (0, 2 * n_heads + h)),
                  pl.BlockSpec((tq, HEAD_DIM), lambda h, i: (i, 0)),
                  pl.BlockSpec((1, t // tq, 1, tq), lambda h, i: (h, 0, 0, 0))],
        out_specs=pl.BlockSpec((tq, HEAD_DIM), lambda h, i: (i, h)),
        out_shape=jax.ShapeDtypeStruct((t, n_heads * HEAD_DIM), BF16),
        scratch_shapes=[pltpu.VMEM((8, HEAD_DIM), F32)],
        compiler_params=_params(("arbitrary", "arbitrary"), 40),
        name="fox_attention",
    )(proj, proj, proj, cum, fk)


def _mem_attn_kernel(q_ref, k_ref, v_ref, o_ref):
    s = _dot_nt(q_ref[...], k_ref[...])
    m = jnp.max(s, axis=1, keepdims=True)
    p = jnp.exp(s - m)
    l = jnp.sum(p, axis=1, keepdims=True)
    o_ref[...] = (_dot(p.astype(BF16), v_ref[...]) * (1.0 / l)).astype(o_ref.dtype)


def _memory_attention(proj, kv, q_col0):
    t = proj.shape[0]
    n_mem = kv.shape[0]
    tq = min(1024, t)
    c0 = q_col0 // HEAD_DIM
    return pl.pallas_call(
        _mem_attn_kernel,
        grid=(MEM_HEADS, t // tq),
        in_specs=[pl.BlockSpec((tq, HEAD_DIM), lambda h, i: (i, c0 + h)),
                  pl.BlockSpec((n_mem, HEAD_DIM), lambda h, i: (0, h)),
                  pl.BlockSpec((n_mem, HEAD_DIM), lambda h, i: (0, MEM_HEADS + h))],
        out_specs=pl.BlockSpec((tq, HEAD_DIM), lambda h, i: (i, h)),
        out_shape=jax.ShapeDtypeStruct((t, MEM_HEADS * HEAD_DIM), BF16),
        compiler_params=_params(("arbitrary", "arbitrary"), 32),
        name="memory_attention",
    )(proj, kv, kv)


def _out_proj_kernel(om_ref, oc_ref, w_ref, x_ref, g_ref, b_ref, y_ref, yp_ref):
    n_mix = om_ref.shape[1]
    mix = _dot(om_ref[...], w_ref[:n_mix, :]) + _dot(oc_ref[...], w_ref[n_mix:, :])
    y = _layer_norm(ALPHA * x_ref[...] + mix, g_ref[...], b_ref[...])
    y_ref[...] = y
    _store_rows(yp_ref, y)


def _out_proj_norm(o_mix, o_mem, w_o_bf, x, g, b):
    t, d = x.shape
    tm = min(512, t)
    n_mix, n_mem = o_mix.shape[1], o_mem.shape[1]
    assert d == ROW_TILE * HEAD_DIM
    return pl.pallas_call(
        _out_proj_kernel,
        grid=(t // tm,),
        in_specs=[pl.BlockSpec((tm, n_mix), lambda i: (i, 0)),
                  pl.BlockSpec((tm, n_mem), lambda i: (i, 0)),
                  pl.BlockSpec((n_mix + n_mem, d), lambda i: (0, 0)),
                  pl.BlockSpec((tm, d), lambda i: (i, 0)),
                  pl.BlockSpec((1, d), lambda i: (0, 0)),
                  pl.BlockSpec((1, d), lambda i: (0, 0))],
        out_specs=[pl.BlockSpec((tm, d), lambda i: (i, 0)),
                   pl.BlockSpec((tm * ROW_TILE, HEAD_DIM), lambda i: (i, 0))],
        out_shape=[jax.ShapeDtypeStruct((t, d), F32),
                   jax.ShapeDtypeStruct((t * ROW_TILE, HEAD_DIM), F32)],
        compiler_params=_params(("arbitrary",), 48),
        name="out_proj_norm",
    )(o_mix, o_mem, w_o_bf, x, g.reshape(1, d), b.reshape(1, d))


def _router_kernel(x_ref, wh_ref, wl_ref, b_ref, eid_ref, rank_ref, gate_ref, cnt_ref, carry_ref):
    tm = x_ref.shape[0]

    @pl.when(pl.program_id(0) == 0)
    def _():
        carry_ref[...] = jnp.zeros_like(carry_ref)

    xh, xl = _split2(x_ref[...])
    wh, wl = wh_ref[...], wl_ref[...]
    logits = _dot_nt(wh, xh) + _dot_nt(wh, xl) + _dot_nt(wl, xh)
    scores = 1.0 / (1.0 + jnp.exp(-logits))
    sel = scores + b_ref[:, 0:1]

    group_score = []
    for g in range(N_GROUPS):
        v = sel[g * GROUP_SIZE:(g + 1) * GROUP_SIZE, :]
        m1 = jnp.max(v, axis=0, keepdims=True)
        is_max = v == m1
        n_max = jnp.sum(is_max.astype(F32), axis=0, keepdims=True)
        m2 = jnp.max(jnp.where(is_max, NEG_INF, v), axis=0, keepdims=True)
        group_score.append(m1 + jnp.where(n_max >= 2.0, m1, m2))
    masked = []
    for g in range(N_GROUPS):
        ahead = jnp.zeros((1, tm), F32)
        for g2 in range(N_GROUPS):
            if g2 == g:
                continue
            beats = group_score[g2] > group_score[g]
            if g2 < g:
                beats = jnp.logical_or(beats, group_score[g2] == group_score[g])
            ahead = ahead + beats.astype(F32)
        v = sel[g * GROUP_SIZE:(g + 1) * GROUP_SIZE, :]
        masked.append(jnp.where(ahead < float(TOPK_GROUPS), v, NEG_INF))
    msel = jnp.concatenate(masked, axis=0)

    e_idx = lax.broadcasted_iota(I32, (N_EXPERTS, tm), 0)
    ahead = jnp.zeros((N_EXPERTS, tm), F32)
    for e2 in range(N_EXPERTS):
        other = msel[e2:e2 + 1, :]
        beats = jnp.logical_or(other > msel, jnp.logical_and(other == msel, e_idx > e2))
        ahead = ahead + beats.astype(F32)
    chosen = ahead < float(TOP_K)
    chosen_f = chosen.astype(F32)

    w = jnp.where(chosen, scores, 0.0)
    gates = w / jnp.sum(w, axis=0, keepdims=True) * ROUTED_SCALE

    chosen_b = chosen_f.astype(BF16)
    r64 = lax.broadcasted_iota(I32, (N_EXPERTS, N_EXPERTS), 0)
    c64 = lax.broadcasted_iota(I32, (N_EXPERTS, N_EXPERTS), 1)
    choice = _dot((c64 < r64).astype(BF16), chosen_b)
    rt = lax.broadcasted_iota(I32, (tm, tm), 0)
    ct = lax.broadcasted_iota(I32, (tm, tm), 1)
    rank = _dot(chosen_b, (rt < ct).astype(BF16)) + carry_ref[:, 0:1]
    carry_ref[...] = carry_ref[...] + jnp.sum(chosen_f, axis=1, keepdims=True)
    cnt_ref[...] = carry_ref[...].astype(I32)

    e_f = e_idx.astype(F32)
    eids, ranks, gts = [], [], []
    for k in range(TOP_K):
        pick = jnp.logical_and(chosen, choice == float(k))
        eids.append(jnp.sum(jnp.where(pick, e_f, 0.0), axis=0, keepdims=True))
        ranks.append(jnp.sum(jnp.where(pick, rank, 0.0), axis=0, keepdims=True))
        gts.append(jnp.sum(jnp.where(pick, gates, 0.0), axis=0, keepdims=True))
    eid_ref[...] = jnp.concatenate(eids, axis=0).astype(I32)
    rank_ref[...] = jnp.concatenate(ranks, axis=0).astype(I32)
    gate_ref[...] = jnp.concatenate(gts, axis=0)


def _router(x1, w_router, router_bias):
    t, d = x1.shape
    tm = min(512, t)
    wt = w_router.T
    wh, wl = _split2(wt)
    bias = jnp.broadcast_to(router_bias.astype(F32)[:, None], (N_EXPERTS, HEAD_DIM))
    return pl.pallas_call(
        _router_kernel,
        grid=(t // tm,),
        in_specs=[pl.BlockSpec((tm, d), lambda i: (i, 0)),
                  pl.BlockSpec((N_EXPERTS, d), lambda i: (0, 0)),
                  pl.BlockSpec((N_EXPERTS, d), lambda i: (0, 0)),
                  pl.BlockSpec((N_EXPERTS, HEAD_DIM), lambda i: (0, 0))],
        out_specs=[pl.BlockSpec((TOP_K, tm), lambda i: (0, i)),
                   pl.BlockSpec((TOP_K, tm), lambda i: (0, i)),
                   pl.BlockSpec((TOP_K, tm), lambda i: (0, i)),
                   pl.BlockSpec((N_EXPERTS, HEAD_DIM), lambda i: (0, 0))],
        out_shape=[jax.ShapeDtypeStruct((TOP_K, t), I32),
                   jax.ShapeDtypeStruct((TOP_K, t), I32),
                   jax.ShapeDtypeStruct((TOP_K, t), F32),
                   jax.ShapeDtypeStruct((N_EXPERTS, HEAD_DIM), I32)],
        scratch_shapes=[pltpu.VMEM((N_EXPERTS, HEAD_DIM), F32)],
        compiler_params=_params(("arbitrary",), 40),
        name="router",
    )(x1, wh, wl, bias)


def _expert_kernel(te_ref, nu_ref, meta_hbm, x_hbm, wgu_ref, wd_ref, y_hbm,
                   meta_s, xa, xb, ya, yb, y_stage, wgu_bf, wd_bf, msem, gsem, ssem, *, n_tok):
    tm = xa.shape[0] // ROW_TILE
    ff = wd_ref.shape[0]
    j = pl.program_id(0)
    nu = nu_ref[0]

    def meta_copy(a):
        slot = a % META_RING
        return pltpu.make_async_copy(meta_hbm.at[pl.ds(a * 2 * tm, 2 * tm)],
                                     meta_s.at[pl.ds(slot * 2 * tm, 2 * tm)], msem.at[slot])

    def start_gathers(a, x_dst, sem):
        base = (a % META_RING) * 2 * tm
        for r in range(tm):
            src = pl.multiple_of(meta_s[base + r], ROW_TILE)
            pltpu.async_copy(x_hbm.at[pl.ds(src, ROW_TILE), :],
                             x_dst.at[pl.ds(r * ROW_TILE, ROW_TILE), :], sem, priority=r % 2)

    def start_scatters(a, y_src, sem):
        base = (a % META_RING) * 2 * tm + tm
        for r in range(tm):
            dest = pl.multiple_of(meta_s[base + r], ROW_TILE)
            pltpu.async_copy(y_src.at[pl.ds(r * ROW_TILE, ROW_TILE), :],
                             y_hbm.at[pl.ds(dest, ROW_TILE), :], sem, priority=r % 2)

    def wait_gathers(x_dst, sem):
        pltpu.make_async_copy(x_hbm.at[pl.ds(0, tm * ROW_TILE), :], x_dst, sem).wait()

    def wait_scatters(y_src, sem):
        pltpu.make_async_copy(y_src, y_hbm.at[pl.ds(0, tm * ROW_TILE), :], sem).wait()

    @pl.when(j == 0)
    def _():
        ya[...] = jnp.zeros_like(ya)
        yb[...] = jnp.zeros_like(yb)
        pltpu.make_async_copy(ya, y_hbm.at[pl.ds(TOP_K * n_tok * ROW_TILE, tm * ROW_TILE), :],
                              ssem.at[0]).start()
        meta_copy(0).start()
        meta_copy(1).start()
        meta_copy(2).start()
        meta_copy(0).wait()
        meta_copy(1).wait()
        start_gathers(1, xa, gsem.at[0])

    def step(x_cur, y_cur, x_nxt, y_prv, p):
        wait_gathers(x_cur, gsem.at[p])
        wait_scatters(y_cur, ssem.at[p])
        start_gathers(j + 2, x_nxt, gsem.at[1 - p])
        start_scatters(j, y_prv, ssem.at[1 - p])
        gu = _dot(_load_rows(x_cur).astype(BF16), wgu_bf[...])
        a = gu[:, :ff]
        u = gu[:, ff:]
        hidden = (a * (1.0 / (1.0 + jnp.exp(-a))) * u).astype(BF16)
        _store_rows(y_stage, _dot(hidden, wd_bf[...]))
        y_cur[...] = y_stage[...].astype(y_cur.dtype)

        @pl.when(j == nu - 1)
        def _():
            start_scatters(j + 1, y_cur, ssem.at[p])
            wait_scatters(y_prv, ssem.at[1 - p])
            wait_scatters(y_cur, ssem.at[p])
            wait_gathers(x_nxt, gsem.at[1 - p])
            meta_copy(j + 3).wait()

    @pl.when(j < nu)
    def _():
        meta_copy(j + 3).start()
        meta_copy(j + 2).wait()
        new_expert = jnp.logical_or(j == 0, te_ref[j] != te_ref[jnp.maximum(j - 1, 0)])

        @pl.when(new_expert)
        def _():
            wgu_bf[...] = wgu_ref[...].astype(BF16)
            wd_bf[...] = wd_ref[...].astype(BF16)

        @pl.when(j % 2 == 0)
        def _():
            step(xa, ya, xb, yb, 0)

        @pl.when(j % 2 == 1)
        def _():
            step(xb, yb, xa, ya, 1)


def _routed_experts(x_rows, meta, tile_expert, n_used, w_gu, w_down):
    t = x_rows.shape[0] // ROW_TILE
    d = w_gu.shape[1]
    ff = w_down.shape[1]
    tm = EXPERT_TILE
    n_tiles = tile_expert.shape[0]
    kern = functools.partial(_expert_kernel, n_tok=t)
    row_buf = pltpu.VMEM((tm * ROW_TILE, HEAD_DIM), F32)
    out_buf = pltpu.VMEM((tm * ROW_TILE, HEAD_DIM), BF16)
    grid_spec = pltpu.PrefetchScalarGridSpec(
        num_scalar_prefetch=2,
        grid=(n_tiles,),
        in_specs=[pl.BlockSpec(memory_space=pl.ANY),
                  pl.BlockSpec(memory_space=pl.ANY),
                  pl.BlockSpec((None, d, 2 * ff), lambda j, te, nu: (te[j], 0, 0)),
                  pl.BlockSpec((None, ff, d), lambda j, te, nu: (te[j], 0, 0))],
        out_specs=pl.BlockSpec(memory_space=pl.ANY),
        scratch_shapes=[pltpu.SMEM((META_RING * 2 * tm,), I32),
                        row_buf, row_buf, out_buf, out_buf, row_buf,
                        pltpu.VMEM((d, 2 * ff), BF16),
                        pltpu.VMEM((ff, d), BF16),
                        pltpu.SemaphoreType.DMA((META_RING,)),
                        pltpu.SemaphoreType.DMA((2,)),
                        pltpu.SemaphoreType.DMA((2,))],
    )
    return pl.pallas_call(
        kern,
        grid_spec=grid_spec,
        out_shape=jax.ShapeDtypeStruct(((TOP_K * t + 2 * tm) * ROW_TILE, HEAD_DIM), BF16),
        compiler_params=_params(("arbitrary",), 56),
        name="routed_experts",
    )(tile_expert, n_used, meta, x_rows, w_gu, w_down)


def _invert_kernel(slot_ref, zeros_hbm, out_ref, sem):
    unroll = INVERT_UNROLL
    step = pl.program_id(0)
    n_place = slot_ref.shape[0] // INVERT_STEPS

    @pl.when(step == 0)
    def _():
        fill = pltpu.make_async_copy(zeros_hbm, out_ref, sem)
        fill.start()
        fill.wait()

    base = step * n_place

    def place(i, _):
        for u in range(unroll):
            src = base + i * unroll + u
            out_ref[slot_ref[src]] = src
        return 0
    lax.fori_loop(0, n_place // unroll, place, 0)


def _invert_slots(slot, n_slots):
    assert slot.shape[0] % (INVERT_STEPS * INVERT_UNROLL) == 0
    return pl.pallas_call(
        _invert_kernel,
        grid=(INVERT_STEPS,),
        in_specs=[pl.BlockSpec(memory_space=pltpu.SMEM),
                  pl.BlockSpec(memory_space=pl.ANY)],
        out_specs=pl.BlockSpec(memory_space=pltpu.SMEM),
        out_shape=jax.ShapeDtypeStruct((n_slots,), I32),
        scratch_shapes=[pltpu.SemaphoreType.DMA(())],
        compiler_params=pltpu.CompilerParams(dimension_semantics=("arbitrary",)),
        name="invert_slots",
    )(slot, jnp.zeros((n_slots,), I32))


def _dispatch_plan(eid, rank, counts, t):
    tm = EXPERT_TILE
    n_tiles = (TOP_K * t) // tm + N_EXPERTS
    n_slots = n_tiles * tm
    assert t & (t - 1) == 0
    padded = ((counts + tm - 1) // tm) * tm
    ends = jnp.cumsum(padded)
    offs = ends - padded
    n_used = (ends[-1] // tm).astype(I32)
    tiles = jnp.arange(n_tiles, dtype=I32)
    experts = jnp.arange(N_EXPERTS, dtype=I32)
    tile_start = jnp.minimum(tiles, n_used - 1) * tm
    tile_expert = jnp.sum((ends[None, :] <= tile_start[:, None]).astype(I32), axis=1)
    tile_expert = jnp.clip(tile_expert, 0, N_EXPERTS - 1)
    of_tile = tile_expert[:, None] == experts[None, :]
    tile_count = jnp.sum(jnp.where(of_tile, counts[None, :], 0), axis=1)
    tile_offs = jnp.sum(jnp.where(of_tile, offs[None, :], 0), axis=1)
    tile_valid = jnp.clip(tile_count - (tiles * tm - tile_offs), 0, tm).astype(I32)
    slot = rank + jnp.sum(jnp.where(eid[:, :, None] == experts, offs, 0), axis=2)
    packed = _invert_slots(slot.reshape(-1), n_slots).reshape(n_tiles, tm)
    row = lax.broadcasted_iota(I32, (n_tiles, tm), 1)
    gather_rows = packed & (t - 1)
    spare = TOP_K * t + (tiles[:, None] % 2) * tm + row
    scatter_rows = jnp.where(row < tile_valid[:, None], packed, spare)
    plan = jnp.stack([gather_rows, scatter_rows], axis=1)
    dummy = jnp.stack([jnp.zeros((1, tm), I32), TOP_K * t + tm + row[:1]], axis=1)
    tail = jnp.concatenate([dummy, dummy], axis=0)
    meta = jnp.concatenate([dummy, plan, tail], axis=0).reshape(-1) * ROW_TILE
    return meta, tile_expert, n_used.reshape(1)


def _combine_kernel(x_ref, gate_ref, wgu_ref, wd_ref, g_ref, b_ref, *refs):
    y_refs = refs[:TOP_K]
    o_ref, ob_ref, stage_ref = refs[TOP_K:]
    tm = x_ref.shape[0]
    ff = wd_ref.shape[0]
    x = x_ref[...]
    gu = _dot(x.astype(BF16), wgu_ref[...])
    a = gu[:, :ff]
    u = gu[:, ff:]
    hidden = (a * (1.0 / (1.0 + jnp.exp(-a))) * u).astype(BF16)
    ffn = _dot(hidden, wd_ref[...])
    rows = tm * ROW_TILE
    gates = gate_ref[...]
    routed = jnp.zeros((rows, HEAD_DIM), F32)
    for k in range(TOP_K):
        gate_lanes = jnp.broadcast_to(gates[:, k:k + 1], (tm, HEAD_DIM))
        gate_rows = jnp.broadcast_to(gate_lanes[:, None, :], (tm, ROW_TILE, HEAD_DIM))
        routed = routed + gate_rows.reshape(rows, HEAD_DIM) * y_refs[k][...].astype(F32)
    stage_ref[...] = routed
    out = _layer_norm(ALPHA * x + (ffn + _load_rows(stage_ref)), g_ref[...], b_ref[...])
    o_ref[...] = out
    ob_ref[...] = out.astype(BF16)


def _combine_norm(x1, gates_tk, y, w_gu_bf, w_down_bf, g, b):
    t, d = x1.shape
    ff = w_down_bf.shape[0]
    tm = min(128, t)
    nb = t // tm
    y_specs = [pl.BlockSpec((tm * ROW_TILE, HEAD_DIM), functools.partial(lambda i, k: (k * nb + i, 0), k=k))
               for k in range(TOP_K)]
    return pl.pallas_call(
        _combine_kernel,
        grid=(nb,),
        in_specs=[pl.BlockSpec((tm, d), lambda i: (i, 0)),
                  pl.BlockSpec((tm, TOP_K), lambda i: (i, 0)),
                  pl.BlockSpec((d, 2 * ff), lambda i: (0, 0)),
                  pl.BlockSpec((ff, d), lambda i: (0, 0)),
                  pl.BlockSpec((1, d), lambda i: (0, 0)),
                  pl.BlockSpec((1, d), lambda i: (0, 0))] + y_specs,
        out_specs=[pl.BlockSpec((tm, d), lambda i: (i, 0)),
                   pl.BlockSpec((tm, d), lambda i: (i, 0))],
        out_shape=[jax.ShapeDtypeStruct((t, d), F32),
                   jax.ShapeDtypeStruct((t, d), BF16)],
        scratch_shapes=[pltpu.VMEM((tm * ROW_TILE, HEAD_DIM), F32)],
        compiler_params=_params(("arbitrary",), 48),
        name="combine_norm",
    )(x1, gates_tk, w_gu_bf, w_down_bf, g.reshape(1, d), b.reshape(1, d), *([y] * TOP_K))


def _moe_block(x1, x1_rows, router, router_bias, exp_w_gu, exp_w_down, shared_w_gu, shared_w_down,
               g, b):
    t = x1.shape[0]
    eid, rank, gate, counts = _router(x1, router, router_bias)
    meta, tile_expert, n_used = _dispatch_plan(eid, rank, counts[:, 0], t)
    y = _routed_experts(x1_rows, meta, tile_expert, n_used, exp_w_gu, exp_w_down)
    return _combine_norm(x1, gate.T, y, shared_w_gu.astype(BF16), shared_w_down.astype(BF16), g, b)


def _layer(x, xb, kv, mixer, w_in, forget_bias, w_o, ln_attn_g, ln_attn_b, router, router_bias,
           exp_w_gu, exp_w_down, shared_w_gu, shared_w_down, ln_ffn_g, ln_ffn_b):
    mix_w = MIX_HEADS * HEAD_DIM
    n_proj = 3 * mix_w + MEM_HEADS * HEAD_DIM
    proj = _in_proj(xb, w_in, n_proj, mix_w, 3 * mix_w)
    if mixer == 0:
        o_mix = _stick_attention(proj, MIX_HEADS)
    else:
        cum, cum_t = _forget_cumsum(x, w_in[:, n_proj:], forget_bias)
        o_mix = _fox_attention(proj, cum, cum_t, MIX_HEADS)
    o_mem = _memory_attention(proj, kv, 3 * mix_w)
    x1, x1_rows = _out_proj_norm(o_mix, o_mem, w_o.astype(BF16), x, ln_attn_g, ln_attn_b)
    return _moe_block(x1, x1_rows, router, router_bias, exp_w_gu, exp_w_down, shared_w_gu,
                      shared_w_down, ln_ffn_g, ln_ffn_b)


def kernel(x, mem, mem_ln_g, mem_ln_b, w_mem_kv,
           l0_w_in, l0_w_o, l0_ln_attn_g, l0_ln_attn_b, l0_router, l0_router_bias,
           l0_exp_w_gu, l0_exp_w_down, l0_shared_w_gu, l0_shared_w_down, l0_ln_ffn_g, l0_ln_ffn_b,
           l1_w_in, l1_forget_bias, l1_w_o, l1_ln_attn_g, l1_ln_attn_b, l1_router, l1_router_bias,
           l1_exp_w_gu, l1_exp_w_down, l1_shared_w_gu, l1_shared_w_down, l1_ln_ffn_g, l1_ln_ffn_b):
    batch, seq, d = x.shape
    assert batch == 1
    kv = _memory_kv(mem[0], mem_ln_g, mem_ln_b, w_mem_kv)
    x2 = x[0]
    x2, xb = _layer(x2, x2.astype(BF16), kv, 0, l0_w_in, None, l0_w_o, l0_ln_attn_g, l0_ln_attn_b,
                    l0_router, l0_router_bias, l0_exp_w_gu, l0_exp_w_down, l0_shared_w_gu,
                    l0_shared_w_down, l0_ln_ffn_g, l0_ln_ffn_b)
    x2, _ = _layer(x2, xb, kv, 1, l1_w_in, l1_forget_bias, l1_w_o, l1_ln_attn_g, l1_ln_attn_b,
                   l1_router, l1_router_bias, l1_exp_w_gu, l1_exp_w_down, l1_shared_w_gu,
                   l1_shared_w_down, l1_ln_ffn_g, l1_ln_ffn_b)
    return x2.reshape(batch, seq, d)
```

```python
import functools

import jax
import jax.numpy as jnp
from jax import lax
from jax.experimental import pallas as pl
from jax.experimental.pallas import tpu as pltpu

F32 = jnp.float32
BF16 = jnp.bfloat16
I32 = jnp.int32

HEAD_DIM = 128
MIX_HEADS = 12
MEM_HEADS = 4
N_EXPERTS = 64
TOP_K = 8
N_GROUPS = 8
GROUP_SIZE = N_EXPERTS // N_GROUPS
TOPK_GROUPS = 4
ROUTED_SCALE = 2.5
DEPTH = 2
ALPHA = float((2 * DEPTH) ** 0.25)
LN_EPS = 1e-5

MIB = 1024 * 1024
ATT_TILE = 256
FOX_TILE = 512
STICK_HEADS = 2
EXPERT_TILE = 256
META_RING = 4
ROW_TILE = 16
INVERT_UNROLL = 32
INVERT_STEPS = 16
EXP_ZERO = -104.0
NEG_INF = float("-inf")


def _params(semantics, vmem_mib):
    return pltpu.CompilerParams(dimension_semantics=semantics, vmem_limit_bytes=vmem_mib * MIB)


def _dot(a, b):
    return jnp.dot(a, b, preferred_element_type=F32)


def _dot_nt(a, b):
    return lax.dot_general(a, b, (((1,), (1,)), ((), ())), preferred_element_type=F32)


def _split2(x):
    hi = x.astype(BF16)
    lo = (x - hi.astype(F32)).astype(BF16)
    return hi, lo


def _split3(x):
    hi = x.astype(BF16)
    r = x - hi.astype(F32)
    mid = r.astype(BF16)
    lo = (r - mid.astype(F32)).astype(BF16)
    return hi, mid, lo


def _store_rows(ref, y):
    rows = y.shape[0]
    for c in range(ROW_TILE):
        ref[pl.ds(c, rows, stride=ROW_TILE), :] = y[:, c * HEAD_DIM:(c + 1) * HEAD_DIM]


def _load_rows(ref):
    rows = ref.shape[0] // ROW_TILE
    return jnp.concatenate([ref[pl.ds(c, rows, stride=ROW_TILE), :] for c in range(ROW_TILE)],
                           axis=1)


def _log_sigmoid(z):
    return jnp.minimum(z, 0.0) - jnp.log(1.0 + jnp.exp(-jnp.abs(z)))


def _layer_norm(y, g, b):
    mu = jnp.mean(y, axis=-1, keepdims=True)
    d = y - mu
    var = jnp.mean(d * d, axis=-1, keepdims=True)
    return d * lax.rsqrt(var + LN_EPS) * g + b


def _kv_kernel(mem_ref, g_ref, b_ref, w_ref, o_ref):
    y = _layer_norm(mem_ref[...], g_ref[...], b_ref[...])
    o_ref[...] = _dot(y.astype(BF16), w_ref[...].astype(BF16)).astype(o_ref.dtype)


def _memory_kv(mem, g, b, w):
    n, d = mem.shape
    nw = w.shape[1]
    tn = 512
    return pl.pallas_call(
        _kv_kernel,
        grid=(nw // tn,),
        in_specs=[pl.BlockSpec((n, d), lambda j: (0, 0)),
                  pl.BlockSpec((1, d), lambda j: (0, 0)),
                  pl.BlockSpec((1, d), lambda j: (0, 0)),
                  pl.BlockSpec((d, tn), lambda j: (0, j))],
        out_specs=pl.BlockSpec((n, tn), lambda j: (0, j)),
        out_shape=jax.ShapeDtypeStruct((n, nw), BF16),
        compiler_params=_params(("arbitrary",), 32),
        name="memory_kv",
    )(mem, g.reshape(1, d), b.reshape(1, d), w)


def _in_proj_kernel(x_ref, w_ref, o_ref, wbf_ref, *, k_col0, qmem_col0, scale):
    j = pl.program_id(0)
    tn = o_ref.shape[1]

    @pl.when(pl.program_id(1) == 0)
    def _():
        wbf_ref[...] = w_ref[...].astype(BF16)

    col0 = j * tn
    is_query = jnp.logical_or(col0 < k_col0, col0 >= qmem_col0)
    factor = jnp.where(is_query, scale, 1.0).astype(F32)
    o_ref[...] = (_dot(x_ref[...], wbf_ref[...]) * factor).astype(o_ref.dtype)


def _in_proj(xb, w, n_out, k_col0, qmem_col0):
    m, d = xb.shape
    tm, tn = 1024, 512
    tm = min(tm, m)
    assert k_col0 % tn == 0 and qmem_col0 % tn == 0
    kern = functools.partial(_in_proj_kernel, k_col0=k_col0, qmem_col0=qmem_col0,
                             scale=HEAD_DIM ** -0.5)
    return pl.pallas_call(
        kern,
        grid=(n_out // tn, m // tm),
        in_specs=[pl.BlockSpec((tm, d), lambda j, i: (i, 0)),
                  pl.BlockSpec((d, tn), lambda j, i: (0, j))],
        out_specs=pl.BlockSpec((tm, tn), lambda j, i: (i, j)),
        out_shape=jax.ShapeDtypeStruct((m, n_out), BF16),
        scratch_shapes=[pltpu.VMEM((d, tn), BF16)],
        compiler_params=_params(("arbitrary", "arbitrary"), 40),
        name="in_proj",
    )(xb, w)


def _gate_kernel(x_ref, w_ref, b_ref, cum_ref, cum_t_ref, carry_ref):
    tm = x_ref.shape[0]

    @pl.when(pl.program_id(0) == 0)
    def _():
        carry_ref[...] = jnp.zeros_like(carry_ref)

    xh, xl = _split2(x_ref[...])
    wh, wl = _split2(w_ref[...])
    f = _dot(xh, wh) + _dot(xh, wl) + _dot(xl, wh) + b_ref[...]
    lf = _log_sigmoid(f)
    row = lax.broadcasted_iota(I32, (tm, tm), 0)
    col = lax.broadcasted_iota(I32, (tm, tm), 1)
    lower = (col <= row).astype(BF16)
    p0, p1, p2 = _split3(lf)
    cum = _dot(lower, p0) + _dot(lower, p1) + _dot(lower, p2) + carry_ref[...]
    carry_ref[...] = cum[tm - 1:tm, :]
    cum_ref[...] = cum
    cum_t_ref[...] = cum.T[:cum_t_ref.shape[0], :]


def _forget_cumsum(x, w_f, bias):
    t, d = x.shape
    h = w_f.shape[1]
    tm = min(256, t)
    w_pad = jnp.zeros((d, HEAD_DIM), F32).at[:, :h].set(w_f)
    b_pad = jnp.zeros((1, HEAD_DIM), F32).at[0, :h].set(bias)
    return pl.pallas_call(
        _gate_kernel,
        grid=(t // tm,),
        in_specs=[pl.BlockSpec((tm, d), lambda i: (i, 0)),
                  pl.BlockSpec((d, HEAD_DIM), lambda i: (0, 0)),
                  pl.BlockSpec((1, HEAD_DIM), lambda i: (0, 0))],
        out_specs=[pl.BlockSpec((tm, HEAD_DIM), lambda i: (i, 0)),
                   pl.BlockSpec((16, tm), lambda i: (0, i))],
        out_shape=[jax.ShapeDtypeStruct((t, HEAD_DIM), F32),
                   jax.ShapeDtypeStruct((16, t), F32)],
        scratch_shapes=[pltpu.VMEM((1, HEAD_DIM), F32)],
        compiler_params=_params(("arbitrary",), 32),
        name="forget_cumsum",
    )(x, w_pad, b_pad)


def _stick_kernel(q_ref, k_ref, v_ref, o_ref):
    tq = q_ref.shape[0]
    i = pl.program_id(1)
    row = lax.broadcasted_iota(I32, (tq, tq), 0)
    col = lax.broadcasted_iota(I32, (tq, tq), 1)
    strict = col < row
    later = (row > col).astype(BF16)

    def block(j, c, acc, masked, head):
        start = pl.multiple_of(j * tq, tq)
        lanes = pl.ds(head * HEAD_DIM, HEAD_DIM)
        kb = k_ref[pl.ds(start, tq), lanes]
        vb = v_ref[pl.ds(start, tq), lanes]
        z = _dot_nt(q_ref[:, lanes], kb)
        soft = jnp.log(1.0 + jnp.exp(-jnp.abs(z)))
        log_beta = jnp.minimum(z, 0.0) - soft
        log_stay = jnp.minimum(-z, 0.0) - soft
        if masked:
            log_stay = jnp.where(strict, log_stay, 0.0)
        hi, lo = _split2(log_stay)
        log_after = _dot(hi, later) + _dot(lo, later) + c
        w = jnp.exp(log_beta + log_after)
        if masked:
            w = jnp.where(strict, w, 0.0)
        acc = acc + _dot(w.astype(BF16), vb)
        c = c + jnp.sum(log_stay, axis=1, keepdims=True)
        return c, acc

    heads = range(STICK_HEADS)
    c0 = jnp.zeros((tq, 1), F32)
    acc0 = jnp.zeros((tq, HEAD_DIM), F32)
    first = [block(i, c0, acc0, True, head) for head in heads]

    def any_alive(cs):
        worst = cs[0]
        for c in cs[1:]:
            worst = jnp.maximum(worst, c)
        return (jnp.max(worst) > EXP_ZERO).astype(I32)

    def cond(state):
        j, alive, _ = state
        return jnp.logical_and(j >= 0, alive > 0)

    def body(state):
        j, _, per_head = state
        per_head = [block(j, c, acc, False, head) for head, (c, acc) in zip(heads, per_head)]
        return j - 1, any_alive([c for c, _ in per_head]), per_head

    _, _, last = lax.while_loop(cond, body, (i - 1, any_alive([c for c, _ in first]), first))
    for head, (_, acc) in zip(heads, last):
        o_ref[:, pl.ds(head * HEAD_DIM, HEAD_DIM)] = acc.astype(o_ref.dtype)


def _stick_attention(proj, n_heads):
    t = proj.shape[0]
    tq = min(ATT_TILE, t)
    assert n_heads % STICK_HEADS == 0
    groups = n_heads // STICK_HEADS
    width = STICK_HEADS * HEAD_DIM
    return pl.pallas_call(
        _stick_kernel,
        grid=(groups, t // tq),
        in_specs=[pl.BlockSpec((tq, width), lambda h, i: (i, h)),
                  pl.BlockSpec((t, width), lambda h, i: (0, groups + h)),
                  pl.BlockSpec((t, width), lambda h, i: (0, 2 * groups + h))],
        out_specs=pl.BlockSpec((tq, width), lambda h, i: (i, h)),
        out_shape=jax.ShapeDtypeStruct((t, n_heads * HEAD_DIM), BF16),
        compiler_params=_params(("arbitrary", "arbitrary"), 40),
        name="stick_attention",
    )(proj, proj, proj)


def _fox_kernel(q_ref, k_ref, v_ref, fq_ref, fk_ref, o_ref, kmax_ref):
    tq = q_ref.shape[0]
    h = pl.program_id(0)
    i = pl.program_id(1)

    @pl.when(i == 0)
    def _():
        kk = k_ref[...].astype(F32)
        norm2 = jnp.max(jnp.sum(kk * kk, axis=1, keepdims=True), axis=0, keepdims=True)
        kmax_ref[...] = jnp.broadcast_to(jnp.sqrt(norm2), kmax_ref.shape)

    q = q_ref[...]
    lane = lax.broadcasted_iota(I32, fq_ref.shape, 1)
    fq = jnp.sum(jnp.where(lane == h, fq_ref[...], 0.0), axis=1, keepdims=True)
    qf = q.astype(F32)
    reach = jnp.sqrt(jnp.sum(qf * qf, axis=1, keepdims=True)) * kmax_ref[0:1, 0:1] + fq
    row = lax.broadcasted_iota(I32, (tq, tq), 0)
    col = lax.broadcasted_iota(I32, (tq, tq), 1)
    causal = col <= row

    def block(j, m, l, acc, masked):
        start = pl.multiple_of(j * tq, tq)
        kb = k_ref[pl.ds(start, tq), :]
        vb = v_ref[pl.ds(start, tq), :]
        fk = fk_ref[0, j]
        s = _dot_nt(q, kb) + (fq - fk)
        if masked:
            s = jnp.where(causal, s, NEG_INF)
        m_new = jnp.maximum(m, jnp.max(s, axis=1, keepdims=True))
        alpha = jnp.exp(m - m_new)
        p = jnp.exp(s - m_new)
        l = alpha * l + jnp.sum(p, axis=1, keepdims=True)
        acc = alpha * acc + _dot(p.astype(BF16), vb)
        return m_new, l, acc

    m0 = jnp.full((tq, 1), NEG_INF, F32)
    l0 = jnp.zeros((tq, 1), F32)
    acc0 = jnp.zeros((tq, HEAD_DIM), F32)
    m1, l1, acc1 = block(i, m0, l0, acc0, True)

    def alive_before(jb, m):
        last = fk_ref[0, jnp.maximum(jb - 1, 0)][:, tq - 1:tq]
        return (jnp.max(reach - last - m) > EXP_ZERO - 2.0).astype(I32)

    def cond(state):
        jb, alive, _, _, _ = state
        return jnp.logical_and(jb >= 0, alive > 0)

    def body(state):
        jb, _, m, l, acc = state
        m, l, acc = block(jb, m, l, acc, False)
        return jb - 1, alive_before(jb, m), m, l, acc

    _, _, _, l, acc = lax.while_loop(cond, body, (i - 1, alive_before(i, m1), m1, l1, acc1))
    o_ref[...] = (acc * (1.0 / l)).astype(o_ref.dtype)


def _fox_attention(proj, cum, cum_t, n_heads):
    t = proj.shape[0]
    tq = min(FOX_TILE, t)
    fk = cum_t.reshape(cum_t.shape[0], t // tq, 1, tq)
    return pl.pallas_call(
        _fox_kernel,
        grid=(n_heads, t // tq),
        in_specs=[pl.BlockSpec((tq, HEAD_DIM), lambda h, i: (i, h)),
                  pl.BlockSpec((t, HEAD_DIM), lambda h, i: (0, n_heads + h)),
                  pl.BlockSpec((t, HEAD_DIM), lambda h, i: (0, 2 * n_heads + h)),
                  pl.BlockSpec((tq, HEAD_DIM), lambda h, i: (i, 0)),
                  pl.BlockSpec((1, t // tq, 1, tq), lambda h, i: (h, 0, 0, 0))],
        out_specs=pl.BlockSpec((tq, HEAD_DIM), lambda h, i: (i, h)),
        out_shape=jax.ShapeDtypeStruct((t, n_heads * HEAD_DIM), BF16),
        scratch_shapes=[pltpu.VMEM((8, HEAD_DIM), F32)],
        compiler_params=_params(("arbitrary", "arbitrary"), 40),
        name="fox_attention",
    )(proj, proj, proj, cum, fk)


def _mem_attn_kernel(q_ref, k_ref, v_ref, o_ref):
    s = _dot_nt(q_ref[...], k_ref[...])
    m = jnp.max(s, axis=1, keepdims=True)
    p = jnp.exp(s - m)
    l = jnp.sum(p, axis=1, keepdims=True)
    o_ref[...] = (_dot(p.astype(BF16), v_ref[...]) * (1.0 / l)).astype(o_ref.dtype)


def _memory_attention(proj, kv, q_col0):
    t = proj.shape[0]
    n_mem = kv.shape[0]
    tq = min(1024, t)
    c0 = q_col0 // HEAD_DIM
    return pl.pallas_call(
        _mem_attn_kernel,
        grid=(MEM_HEADS, t // tq),
        in_specs=[pl.BlockSpec((tq, HEAD_DIM), lambda h, i: (i, c0 + h)),
                  pl.BlockSpec((n_mem, HEAD_DIM), lambda h, i: (0, h)),
                  pl.BlockSpec((n_mem, HEAD_DIM), lambda h, i: (0, MEM_HEADS + h))],
        out_specs=pl.BlockSpec((tq, HEAD_DIM), lambda h, i: (i, h)),
        out_shape=jax.ShapeDtypeStruct((t, MEM_HEADS * HEAD_DIM), BF16),
        compiler_params=_params(("arbitrary", "arbitrary"), 32),
        name="memory_attention",
    )(proj, kv, kv)


def _out_proj_kernel(om_ref, oc_ref, w_ref, x_ref, g_ref, b_ref, y_ref, yp_ref):
    n_mix = om_ref.shape[1]
    mix = _dot(om_ref[...], w_ref[:n_mix, :]) + _dot(oc_ref[...], w_ref[n_mix:, :])
    y = _layer_norm(ALPHA * x_ref[...] + mix, g_ref[...], b_ref[...])
    y_ref[...] = y
    _store_rows(yp_ref, y)


def _out_proj_norm(o_mix, o_mem, w_o_bf, x, g, b):
    t, d = x.shape
    tm = min(512, t)
    n_mix, n_mem = o_mix.shape[1], o_mem.shape[1]
    assert d == ROW_TILE * HEAD_DIM
    return pl.pallas_call(
        _out_proj_kernel,
        grid=(t // tm,),
        in_specs=[pl.BlockSpec((tm, n_mix), lambda i: (i, 0)),
                  pl.BlockSpec((tm, n_mem), lambda i: (i, 0)),
                  pl.BlockSpec((n_mix + n_mem, d), lambda i: (0, 0)),
                  pl.BlockSpec((tm, d), lambda i: (i, 0)),
                  pl.BlockSpec((1, d), lambda i: (0, 0)),
                  pl.BlockSpec((1, d), lambda i: (0, 0))],
        out_specs=[pl.BlockSpec((tm, d), lambda i: (i, 0)),
                   pl.BlockSpec((tm * ROW_TILE, HEAD_DIM), lambda i: (i, 0))],
        out_shape=[jax.ShapeDtypeStruct((t, d), F32),
                   jax.ShapeDtypeStruct((t * ROW_TILE, HEAD_DIM), F32)],
        compiler_params=_params(("arbitrary",), 48),
        name="out_proj_norm",
    )(o_mix, o_mem, w_o_bf, x, g.reshape(1, d), b.reshape(1, d))


def _router_kernel(x_ref, wh_ref, wl_ref, b_ref, eid_ref, rank_ref, gate_ref, cnt_ref, carry_ref):
    tm = x_ref.shape[0]

    @pl.when(pl.program_id(0) == 0)
    def _():
        carry_ref[...] = jnp.zeros_like(carry_ref)

    xh, xl = _split2(x_ref[...])
    wh, wl = wh_ref[...], wl_ref[...]
    logits = _dot_nt(wh, xh) + _dot_nt(wh, xl) + _dot_nt(wl, xh)
    scores = 1.0 / (1.0 + jnp.exp(-logits))
    sel = scores + b_ref[:, 0:1]

    group_score = []
    for g in range(N_GROUPS):
        v = sel[g * GROUP_SIZE:(g + 1) * GROUP_SIZE, :]
        m1 = jnp.max(v, axis=0, keepdims=True)
        is_max = v == m1
        n_max = jnp.sum(is_max.astype(F32), axis=0, keepdims=True)
        m2 = jnp.max(jnp.where(is_max, NEG_INF, v), axis=0, keepdims=True)
        group_score.append(m1 + jnp.where(n_max >= 2.0, m1, m2))
    masked = []
    for g in range(N_GROUPS):
        ahead = jnp.zeros((1, tm), F32)
        for g2 in range(N_GROUPS):
            if g2 == g:
                continue
            beats = group_score[g2] > group_score[g]
            if g2 < g:
                beats = jnp.logical_or(beats, group_score[g2] == group_score[g])
            ahead = ahead + beats.astype(F32)
        v = sel[g * GROUP_SIZE:(g + 1) * GROUP_SIZE, :]
        masked.append(jnp.where(ahead < float(TOPK_GROUPS), v, NEG_INF))
    msel = jnp.concatenate(masked, axis=0)

    e_idx = lax.broadcasted_iota(I32, (N_EXPERTS, tm), 0)
    ahead = jnp.zeros((N_EXPERTS, tm), F32)
    for e2 in range(N_EXPERTS):
        other = msel[e2:e2 + 1, :]
        beats = jnp.logical_or(other > msel, jnp.logical_and(other == msel, e_idx > e2))
        ahead = ahead + beats.astype(F32)
    chosen = ahead < float(TOP_K)
    chosen_f = chosen.astype(F32)

    w = jnp.where(chosen, scores, 0.0)
    gates = w / jnp.sum(w, axis=0, keepdims=True) * ROUTED_SCALE

    chosen_b = chosen_f.astype(BF16)
    r64 = lax.broadcasted_iota(I32, (N_EXPERTS, N_EXPERTS), 0)
    c64 = lax.broadcasted_iota(I32, (N_EXPERTS, N_EXPERTS), 1)
    choice = _dot((c64 < r64).astype(BF16), chosen_b)
    rt = lax.broadcasted_iota(I32, (tm, tm), 0)
    ct = lax.broadcasted_iota(I32, (tm, tm), 1)
    rank = _dot(chosen_b, (rt < ct).astype(BF16)) + carry_ref[:, 0:1]
    carry_ref[...] = carry_ref[...] + jnp.sum(chosen_f, axis=1, keepdims=True)
    cnt_ref[...] = carry_ref[...].astype(I32)

    e_f = e_idx.astype(F32)
    eids, ranks, gts = [], [], []
    for k in range(TOP_K):
        pick = jnp.logical_and(chosen, choice == float(k))
        eids.append(jnp.sum(jnp.where(pick, e_f, 0.0), axis=0, keepdims=True))
        ranks.append(jnp.sum(jnp.where(pick, rank, 0.0), axis=0, keepdims=True))
        gts.append(jnp.sum(jnp.where(pick, gates, 0.0), axis=0, keepdims=True))
    eid_ref[...] = jnp.concatenate(eids, axis=0).astype(I32)
    rank_ref[...] = jnp.concatenate(ranks, axis=0).astype(I32)
    gate_ref[...] = jnp.concatenate(gts, axis=0)


def _router(x1, w_router, router_bias):
    t, d = x1.shape
    tm = min(512, t)
    wt = w_router.T
    wh, wl = _split2(wt)
    bias = jnp.broadcast_to(router_bias.astype(F32)[:, None], (N_EXPERTS, HEAD_DIM))
    return pl.pallas_call(
        _router_kernel,
        grid=(t // tm,),
        in_specs=[pl.BlockSpec((tm, d), lambda i: (i, 0)),
                  pl.BlockSpec((N_EXPERTS, d), lambda i: (0, 0)),
                  pl.BlockSpec((N_EXPERTS, d), lambda i: (0, 0)),
                  pl.BlockSpec((N_EXPERTS, HEAD_DIM), lambda i: (0, 0))],
        out_specs=[pl.BlockSpec((TOP_K, tm), lambda i: (0, i)),
                   pl.BlockSpec((TOP_K, tm), lambda i: (0, i)),
                   pl.BlockSpec((TOP_K, tm), lambda i: (0, i)),
                   pl.BlockSpec((N_EXPERTS, HEAD_DIM), lambda i: (0, 0))],
        out_shape=[jax.ShapeDtypeStruct((TOP_K, t), I32),
                   jax.ShapeDtypeStruct((TOP_K, t), I32),
                   jax.ShapeDtypeStruct((TOP_K, t), F32),
                   jax.ShapeDtypeStruct((N_EXPERTS, HEAD_DIM), I32)],
        scratch_shapes=[pltpu.VMEM((N_EXPERTS, HEAD_DIM), F32)],
        compiler_params=_params(("arbitrary",), 40),
        name="router",
    )(x1, wh, wl, bias)


def _expert_kernel(te_ref, nu_ref, meta_hbm, x_hbm, wgu_ref, wd_ref, y_hbm,
                   meta_s, xa, xb, ya, yb, y_stage, wgu_bf, wd_bf, msem, gsem, ssem, *, n_tok):
    tm = xa.shape[0] // ROW_TILE
    ff = wd_ref.shape[0]
    j = pl.program_id(0)
    nu = nu_ref[0]

    def meta_copy(a):
        slot = a % META_RING
        return pltpu.make_async_copy(meta_hbm.at[pl.ds(a * 2 * tm, 2 * tm)],
                                     meta_s.at[pl.ds(slot * 2 * tm, 2 * tm)], msem.at[slot])

    def start_gathers(a, x_dst, sem):
        base = (a % META_RING) * 2 * tm
        for r in range(tm):
            src = pl.multiple_of(meta_s[base + r], ROW_TILE)
            pltpu.async_copy(x_hbm.at[pl.ds(src, ROW_TILE), :],
                             x_dst.at[pl.ds(r * ROW_TILE, ROW_TILE), :], sem, priority=0)

    def start_scatters(a, y_src, sem):
        base = (a % META_RING) * 2 * tm + tm
        for r in range(tm):
            dest = pl.multiple_of(meta_s[base + r], ROW_TILE)
            pltpu.async_copy(y_src.at[pl.ds(r * ROW_TILE, ROW_TILE), :],
                             y_hbm.at[pl.ds(dest, ROW_TILE), :], sem, priority=1)

    def wait_gathers(x_dst, sem):
        pltpu.make_async_copy(x_hbm.at[pl.ds(0, tm * ROW_TILE), :], x_dst, sem).wait()

    def wait_scatters(y_src, sem):
        pltpu.make_async_copy(y_src, y_hbm.at[pl.ds(0, tm * ROW_TILE), :], sem).wait()

    @pl.when(j == 0)
    def _():
        ya[...] = jnp.zeros_like(ya)
        yb[...] = jnp.zeros_like(yb)
        pltpu.make_async_copy(ya, y_hbm.at[pl.ds(TOP_K * n_tok * ROW_TILE, tm * ROW_TILE), :],
                              ssem.at[0]).start()
        meta_copy(0).start()
        meta_copy(1).start()
        meta_copy(2).start()
        meta_copy(0).wait()
        meta_copy(1).wait()
        start_gathers(1, xa, gsem.at[0])

    def step(x_cur, y_cur, x_nxt, y_prv, p):
        wait_gathers(x_cur, gsem.at[p])
        wait_scatters(y_cur, ssem.at[p])
        start_gathers(j + 2, x_nxt, gsem.at[1 - p])
        start_scatters(j, y_prv, ssem.at[1 - p])
        gu = _dot(_load_rows(x_cur).astype(BF16), wgu_bf[...])
        a = gu[:, :ff]
        u = gu[:, ff:]
        hidden = (a * (1.0 / (1.0 + jnp.exp(-a))) * u).astype(BF16)
        _store_rows(y_stage, _dot(hidden, wd_bf[...]))
        y_cur[...] = y_stage[...].astype(y_cur.dtype)

        @pl.when(j == nu - 1)
        def _():
            start_scatters(j + 1, y_cur, ssem.at[p])
            wait_scatters(y_prv, ssem.at[1 - p])
            wait_scatters(y_cur, ssem.at[p])
            wait_gathers(x_nxt, gsem.at[1 - p])
            meta_copy(j + 3).wait()

    @pl.when(j < nu)
    def _():
        meta_copy(j + 3).start()
        meta_copy(j + 2).wait()
        new_expert = jnp.logical_or(j == 0, te_ref[j] != te_ref[jnp.maximum(j - 1, 0)])

        @pl.when(new_expert)
        def _():
            wgu_bf[...] = wgu_ref[...].astype(BF16)
            wd_bf[...] = wd_ref[...].astype(BF16)

        @pl.when(j % 2 == 0)
        def _():
            step(xa, ya, xb, yb, 0)

        @pl.when(j % 2 == 1)
        def _():
            step(xb, yb, xa, ya, 1)


def _routed_experts(x_rows, meta, tile_expert, n_used, w_gu, w_down):
    t = x_rows.shape[0] // ROW_TILE
    d = w_gu.shape[1]
    ff = w_down.shape[1]
    tm = EXPERT_TILE
    n_tiles = tile_expert.shape[0]
    kern = functools.partial(_expert_kernel, n_tok=t)
    row_buf = pltpu.VMEM((tm * ROW_TILE, HEAD_DIM), F32)
    out_buf = pltpu.VMEM((tm * ROW_TILE, HEAD_DIM), BF16)
    grid_spec = pltpu.PrefetchScalarGridSpec(
        num_scalar_prefetch=2,
        grid=(n_tiles,),
        in_specs=[pl.BlockSpec(memory_space=pl.ANY),
                  pl.BlockSpec(memory_space=pl.ANY),
                  pl.BlockSpec((None, d, 2 * ff), lambda j, te, nu: (te[j], 0, 0)),
                  pl.BlockSpec((None, ff, d), lambda j, te, nu: (te[j], 0, 0))],
        out_specs=pl.BlockSpec(memory_space=pl.ANY),
        scratch_shapes=[pltpu.SMEM((META_RING * 2 * tm,), I32),
                        row_buf, row_buf, out_buf, out_buf, row_buf,
                        pltpu.VMEM((d, 2 * ff), BF16),
                        pltpu.VMEM((ff, d), BF16),
                        pltpu.SemaphoreType.DMA((META_RING,)),
                        pltpu.SemaphoreType.DMA((2,)),
                        pltpu.SemaphoreType.DMA((2,))],
    )
    return pl.pallas_call(
        kern,
        grid_spec=grid_spec,
        out_shape=jax.ShapeDtypeStruct(((TOP_K * t + 2 * tm) * ROW_TILE, HEAD_DIM), BF16),
        compiler_params=_params(("arbitrary",), 56),
        name="routed_experts",
    )(tile_expert, n_used, meta, x_rows, w_gu, w_down)


def _invert_kernel(slot_ref, zeros_hbm, out_ref, sem):
    unroll = INVERT_UNROLL
    step = pl.program_id(0)
    n_place = slot_ref.shape[0] // INVERT_STEPS

    @pl.when(step == 0)
    def _():
        fill = pltpu.make_async_copy(zeros_hbm, out_ref, sem)
        fill.start()
        fill.wait()

    base = step * n_place

    def place(i, _):
        for u in range(unroll):
            src = base + i * unroll + u
            out_ref[slot_ref[src]] = src
        return 0
    lax.fori_loop(0, n_place // unroll, place, 0)


def _invert_slots(slot, n_slots):
    assert slot.shape[0] % (INVERT_STEPS * INVERT_UNROLL) == 0
    return pl.pallas_call(
        _invert_kernel,
        grid=(INVERT_STEPS,),
        in_specs=[pl.BlockSpec(memory_space=pltpu.SMEM),
                  pl.BlockSpec(memory_space=pl.ANY)],
        out_specs=pl.BlockSpec(memory_space=pltpu.SMEM),
        out_shape=jax.ShapeDtypeStruct((n_slots,), I32),
        scratch_shapes=[pltpu.SemaphoreType.DMA(())],
        compiler_params=pltpu.CompilerParams(dimension_semantics=("arbitrary",)),
        name="invert_slots",
    )(slot, jnp.zeros((n_slots,), I32))


def _dispatch_plan(eid, rank, counts, t):
    tm = EXPERT_TILE
    n_tiles = (TOP_K * t) // tm + N_EXPERTS
    n_slots = n_tiles * tm
    assert t & (t - 1) == 0
    padded = ((counts + tm - 1) // tm) * tm
    ends = jnp.cumsum(padded)
    offs = ends - padded
    n_used = (ends[-1] // tm).astype(I32)
    tiles = jnp.arange(n_tiles, dtype=I32)
    experts = jnp.arange(N_EXPERTS, dtype=I32)
    tile_start = jnp.minimum(tiles, n_used - 1) * tm
    tile_expert = jnp.sum((ends[None, :] <= tile_start[:, None]).astype(I32), axis=1)
    tile_expert = jnp.clip(tile_expert, 0, N_EXPERTS - 1)
    of_tile = tile_expert[:, None] == experts[None, :]
    tile_count = jnp.sum(jnp.where(of_tile, counts[None, :], 0), axis=1)
    tile_offs = jnp.sum(jnp.where(of_tile, offs[None, :], 0), axis=1)
    tile_valid = jnp.clip(tile_count - (tiles * tm - tile_offs), 0, tm).astype(I32)
    slot = rank + jnp.sum(jnp.where(eid[:, :, None] == experts, offs, 0), axis=2)
    packed = _invert_slots(slot.reshape(-1), n_slots).reshape(n_tiles, tm)
    row = lax.broadcasted_iota(I32, (n_tiles, tm), 1)
    gather_rows = packed & (t - 1)
    spare = TOP_K * t + (tiles[:, None] % 2) * tm + row
    scatter_rows = jnp.where(row < tile_valid[:, None], packed, spare)
    plan = jnp.stack([gather_rows, scatter_rows], axis=1)
    dummy = jnp.stack([jnp.zeros((1, tm), I32), TOP_K * t + tm + row[:1]], axis=1)
    tail = jnp.concatenate([dummy, dummy], axis=0)
    meta = jnp.concatenate([dummy, plan, tail], axis=0).reshape(-1) * ROW_TILE
    return meta, tile_expert, n_used.reshape(1)


def _combine_kernel(x_ref, gate_ref, wgu_ref, wd_ref, g_ref, b_ref, *refs):
    y_refs = refs[:TOP_K]
    o_ref, ob_ref, stage_ref = refs[TOP_K:]
    tm = x_ref.shape[0]
    ff = wd_ref.shape[0]
    x = x_ref[...]
    gu = _dot(x.astype(BF16), wgu_ref[...])
    a = gu[:, :ff]
    u = gu[:, ff:]
    hidden = (a * (1.0 / (1.0 + jnp.exp(-a))) * u).astype(BF16)
    ffn = _dot(hidden, wd_ref[...])
    rows = tm * ROW_TILE
    gates = gate_ref[...]
    routed = jnp.zeros((rows, HEAD_DIM), F32)
    for k in range(TOP_K):
        gate_lanes = jnp.broadcast_to(gates[:, k:k + 1], (tm, HEAD_DIM))
        gate_rows = jnp.broadcast_to(gate_lanes[:, None, :], (tm, ROW_TILE, HEAD_DIM))
        routed = routed + gate_rows.reshape(rows, HEAD_DIM) * y_refs[k][...].astype(F32)
    stage_ref[...] = routed
    out = _layer_norm(ALPHA * x + (ffn + _load_rows(stage_ref)), g_ref[...], b_ref[...])
    o_ref[...] = out
    ob_ref[...] = out.astype(BF16)


def _combine_norm(x1, gates_tk, y, w_gu_bf, w_down_bf, g, b):
    t, d = x1.shape
    ff = w_down_bf.shape[0]
    tm = min(128, t)
    nb = t // tm
    y_specs = [pl.BlockSpec((tm * ROW_TILE, HEAD_DIM), functools.partial(lambda i, k: (k * nb + i, 0), k=k))
               for k in range(TOP_K)]
    return pl.pallas_call(
        _combine_kernel,
        grid=(nb,),
        in_specs=[pl.BlockSpec((tm, d), lambda i: (i, 0)),
                  pl.BlockSpec((tm, TOP_K), lambda i: (i, 0)),
                  pl.BlockSpec((d, 2 * ff), lambda i: (0, 0)),
                  pl.BlockSpec((ff, d), lambda i: (0, 0)),
                  pl.BlockSpec((1, d), lambda i: (0, 0)),
                  pl.BlockSpec((1, d), lambda i: (0, 0))] + y_specs,
        out_specs=[pl.BlockSpec((tm, d), lambda i: (i, 0)),
                   pl.BlockSpec((tm, d), lambda i: (i, 0))],
        out_shape=[jax.ShapeDtypeStruct((t, d), F32),
                   jax.ShapeDtypeStruct((t, d), BF16)],
        scratch_shapes=[pltpu.VMEM((tm * ROW_TILE, HEAD_DIM), F32)],
        compiler_params=_params(("arbitrary",), 48),
        name="combine_norm",
    )(x1, gates_tk, w_gu_bf, w_down_bf, g.reshape(1, d), b.reshape(1, d), *([y] * TOP_K))


def _moe_block(x1, x1_rows, router, router_bias, exp_w_gu, exp_w_down, shared_w_gu, shared_w_down,
               g, b):
    t = x1.shape[0]
    eid, rank, gate, counts = _router(x1, router, router_bias)
    meta, tile_expert, n_used = _dispatch_plan(eid, rank, counts[:, 0], t)
    y = _routed_experts(x1_rows, meta, tile_expert, n_used, exp_w_gu, exp_w_down)
    return _combine_norm(x1, gate.T, y, shared_w_gu.astype(BF16), shared_w_down.astype(BF16), g, b)


def _layer(x, xb, kv, mixer, w_in, forget_bias, w_o, ln_attn_g, ln_attn_b, router, router_bias,
           exp_w_gu, exp_w_down, shared_w_gu, shared_w_down, ln_ffn_g, ln_ffn_b):
    mix_w = MIX_HEADS * HEAD_DIM
    n_proj = 3 * mix_w + MEM_HEADS * HEAD_DIM
    proj = _in_proj(xb, w_in, n_proj, mix_w, 3 * mix_w)
    if mixer == 0:
        o_mix = _stick_attention(proj, MIX_HEADS)
    else:
        cum, cum_t = _forget_cumsum(x, w_in[:, n_proj:], forget_bias)
        o_mix = _fox_attention(proj, cum, cum_t, MIX_HEADS)
    o_mem = _memory_attention(proj, kv, 3 * mix_w)
    x1, x1_rows = _out_proj_norm(o_mix, o_mem, w_o.astype(BF16), x, ln_attn_g, ln_attn_b)
    return _moe_block(x1, x1_rows, router, router_bias, exp_w_gu, exp_w_down, shared_w_gu,
                      shared_w_down, ln_ffn_g, ln_ffn_b)


def kernel(x, mem, mem_ln_g, mem_ln_b, w_mem_kv,
           l0_w_in, l0_w_o, l0_ln_attn_g, l0_ln_attn_b, l0_router, l0_router_bias,
           l0_exp_w_gu, l0_exp_w_down, l0_shared_w_gu, l0_shared_w_down, l0_ln_ffn_g, l0_ln_ffn_b,
           l1_w_in, l1_forget_bias, l1_w_o, l1_ln_attn_g, l1_ln_attn_b, l1_router, l1_router_bias,
           l1_exp_w_gu, l1_exp_w_down, l1_shared_w_gu, l1_shared_w_down, l1_ln_ffn_g, l1_ln_ffn_b):
    batch, seq, d = x.shape
    assert batch == 1
    kv = _memory_kv(mem[0], mem_ln_g, mem_ln_b, w_mem_kv)
    x2 = x[0]
    x2, xb = _layer(x2, x2.astype(BF16), kv, 0, l0_w_in, None, l0_w_o, l0_ln_attn_g, l0_ln_attn_b,
                    l0_router, l0_router_bias, l0_exp_w_gu, l0_exp_w_down, l0_shared_w_gu,
                    l0_shared_w_down, l0_ln_ffn_g, l0_ln_ffn_b)
    x2, _ = _layer(x2, xb, kv, 1, l1_w_in, l1_forget_bias, l1_w_o, l1_ln_attn_g, l1_ln_attn_b,
                   l1_router, l1_router_bias, l1_exp_w_gu, l1_exp_w_down, l1_shared_w_gu,
                   l1_shared_w_down, l1_ln_ffn_g, l1_ln_ffn_b)
    return x2.reshape(batch, seq, d)
```
